```python
import math
import jax, jax.numpy as jnp
from jax import lax
import numpy as np

D_MODEL = 2048
BATCH = 2
SEQ = 4096
DEPTH = 1
DEC_BATCH = 32
DEC_SEQ = 4
PAST_LEN = 8192
PAGE_SIZE = 128

C_CONV = D_MODEL // 2
CONV_WIDTH = 31
N_HEADS = 8
HEAD_DIM = D_MODEL // (2 * N_HEADS)
V_DIM = 2 * HEAD_DIM
QK_WIDTH = 2 * N_HEADS * HEAD_DIM
ATT_WIDTH = N_HEADS * V_DIM
N_IN = 3 * C_CONV + 2 * QK_WIDTH + 2 * ATT_WIDTH + 2 * D_MODEL
N_BUCKETS = 32
MAX_DISTANCE = 128
PLE_DIM = 256
Q_BLOCK = 128
LN_EPS = 1e-5
NEG_INF = -1e30
ALPHA = (2 * DEPTH) ** 0.25
BETA = (8 * DEPTH) ** -0.25

kernel_name = 'hybrid_conformer_diffattn_decode_step'


def _layernorm(x, g, b):
    xf = x.astype(jnp.float32)
    mu = jnp.mean(xf, axis=-1, keepdims=True)
    xc = xf - mu
    var = jnp.mean(xc * xc, axis=-1, keepdims=True)
    y = xc * lax.rsqrt(var + LN_EPS) * g.astype(jnp.float32) + b.astype(jnp.float32)
    return y.astype(x.dtype)


def _split_in(xin):
    sizes = (C_CONV, C_CONV, C_CONV, QK_WIDTH, QK_WIDTH, ATT_WIDTH, ATT_WIDTH, D_MODEL, D_MODEL)
    bounds = [int(b) for b in np.cumsum(sizes)[:-1]]
    return jnp.split(xin, bounds, axis=-1)


def _rel_bias(q_pos, k_pos, table):
    n = jnp.maximum(q_pos[:, None] - k_pos[None, :], 0)
    max_exact = N_BUCKETS // 2
    large = max_exact + (jnp.log(jnp.maximum(n, 1).astype(jnp.float32) / max_exact)
                         / math.log(MAX_DISTANCE / max_exact) * (N_BUCKETS - max_exact)).astype(jnp.int32)
    large = jnp.minimum(large, N_BUCKETS - 1)
    bucket = jnp.where(n < max_exact, n, large)
    return jnp.transpose(table[bucket].astype(jnp.float32), (2, 0, 1))


def _diff_logits(q, k, q_pos, k_pos, table):
    b, tq = q.shape[0], q.shape[1]
    tk = k.shape[1]
    s = jnp.einsum('bqgd,bkgd->bgqk', q, k).astype(jnp.float32) * (HEAD_DIM ** -0.5)
    s = s.reshape(b, N_HEADS, 2, tq, tk) + _rel_bias(q_pos, k_pos, table)[None, :, None]
    causal = k_pos[None, :] <= q_pos[:, None]
    return jnp.where(causal, s, NEG_INF)


def _attn_prompt(q, k, v, table, lam):
    b, t = q.shape[0], q.shape[1]
    nb = t // Q_BLOCK
    k_pos = jnp.arange(t)
    qb = q.reshape(b, nb, Q_BLOCK, 2 * N_HEADS, HEAD_DIM).transpose(1, 0, 2, 3, 4)
    starts = jnp.arange(nb) * Q_BLOCK

    def block(args):
        q_blk, start = args
        q_pos = start + jnp.arange(Q_BLOCK)
        p = jax.nn.softmax(_diff_logits(q_blk, k, q_pos, k_pos, table), axis=-1)
        a = (p[:, :, 0] - lam * p[:, :, 1]).astype(v.dtype)
        return jnp.einsum('bhqk,bkhe->bqhe', a, v)

    o = lax.map(block, (qb, starts))
    return o.transpose(1, 0, 2, 3, 4).reshape(b, t, N_HEADS, V_DIM)


def _attn_sample(q, k, v, k_past, v_past, table, lam):
    past = k_past.shape[1]
    t = q.shape[1]
    q_pos = past + jnp.arange(t)
    s_past = _diff_logits(q, k_past, q_pos, jnp.arange(past), table)
    s_new = _diff_logits(q, k, q_pos, q_pos, table)
    p = jax.nn.softmax(jnp.concatenate([s_past, s_new], axis=-1), axis=-1)
    a = (p[:, :, 0] - lam * p[:, :, 1]).astype(v.dtype)
    return (jnp.einsum('bhqk,bkhe->bqhe', a[..., :past], v_past)
            + jnp.einsum('bhqk,bkhe->bqhe', a[..., past:], v))


def _conv_branch(u_a, u_g, z_a, conv_prev, conv_w, conv_b, ln_g, ln_b, w_proj_a):
    g = u_a * jax.nn.sigmoid(u_g)
    xc = jnp.concatenate([conv_prev.astype(g.dtype), g], axis=1)
    h = lax.conv_general_dilated(xc, conv_w[:, None, :].astype(g.dtype), window_strides=(1,), padding='VALID',
                                 dimension_numbers=('NWC', 'WIO', 'NWC'), feature_group_count=C_CONV) + conv_b
    h = jax.nn.silu(_layernorm(h, ln_g, ln_b)) * jax.nn.silu(z_a)
    return h @ w_proj_a, xc[:, -(CONV_WIDTH - 1):]


def _attn_branch(o, z_b, subln_w, lam_init, w_proj_b):
    b, t = o.shape[0], o.shape[1]
    of = o.astype(jnp.float32)
    of = of * lax.rsqrt(jnp.mean(of * of, axis=-1, keepdims=True) + LN_EPS) * subln_w.astype(jnp.float32) * (1.0 - lam_init)
    h = of.astype(o.dtype).reshape(b, t, ATT_WIDTH) * jax.nn.silu(z_b)
    return h @ w_proj_b


def _lambda(lq1, lk1, lq2, lk2, lam_init):
    f = jnp.float32
    return (jnp.exp(jnp.sum(lq1.astype(f) * lk1.astype(f))) - jnp.exp(jnp.sum(lq2.astype(f) * lk2.astype(f))) + lam_init)


def _layer(x, p, conv_prev, attend, w_in, conv_w, conv_b, conv_ln_g, conv_ln_b, w_proj_a, subln_w, lam_init,
           w_proj_b, w_out, ln_g, ln_b, w_ple_proj, w_ple_gate):
    b, t, _ = x.shape
    u_a, u_g, z_a, q, k, v, z_b, g_a, g_b = _split_in(x @ w_in)
    q = q.reshape(b, t, 2 * N_HEADS, HEAD_DIM)
    k = k.reshape(b, t, 2 * N_HEADS, HEAD_DIM)
    v = v.reshape(b, t, N_HEADS, V_DIM)
    branch_a, conv_new = _conv_branch(u_a, u_g, z_a, conv_prev, conv_w, conv_b, conv_ln_g, conv_ln_b, w_proj_a)
    branch_b = _attn_branch(attend(q, k, v), z_b, subln_w, lam_init, w_proj_b)
    merged = jax.nn.sigmoid(g_a) * branch_a + jax.nn.sigmoid(g_b) * branch_b
    h = _layernorm(ALPHA * x + merged @ w_out, ln_g, ln_b)
    y = h + jax.nn.sigmoid(h @ w_ple_gate) * (p @ w_ple_proj)
    return y, k, v, conv_new


def setup_inputs(seed: int = 0) -> dict:
    key = jax.random.key(seed)
    ks = jax.random.split(key, 32)
    f = jnp.float32
    n_pages = PAST_LEN // PAGE_SIZE
    n_used = DEC_BATCH * n_pages
    n_phys = n_used + (n_used + 3) // 4
    page_table = jax.random.permutation(ks[0], n_phys)[:n_used].reshape(DEC_BATCH, n_pages).astype(jnp.int32)
    v_start = 3 * C_CONV + 2 * QK_WIDTH
    col_scale = jnp.ones((N_IN,), f).at[v_start:v_start + ATT_WIDTH].set(BETA)
    nrm = lambda k, s: jax.random.normal(k, s, f)
    return {
        'x_prompt': nrm(ks[1], (BATCH, SEQ, D_MODEL)),
        'x_sample': nrm(ks[2], (DEC_BATCH, DEC_SEQ, D_MODEL)),
        'p_prompt': nrm(ks[3], (DEPTH, BATCH, SEQ, PLE_DIM)),
        'p_sample': nrm(ks[4], (DEPTH, DEC_BATCH, DEC_SEQ, PLE_DIM)),
        'cache_k': nrm(ks[5], (DEPTH, n_phys, PAGE_SIZE, 2 * N_HEADS, HEAD_DIM)),
        'cache_v': nrm(ks[6], (DEPTH, n_phys, PAGE_SIZE, N_HEADS, V_DIM)) * BETA,
        'state_conv': nrm(ks[7], (DEPTH, DEC_BATCH, CONV_WIDTH - 1, C_CONV)) * 0.5,
        'page_table': page_table,
        'w_in': nrm(ks[8], (DEPTH, D_MODEL, N_IN)) * (D_MODEL ** -0.5) * col_scale,
        'conv_w': nrm(ks[9], (DEPTH, CONV_WIDTH, C_CONV)) * (CONV_WIDTH ** -0.5),
        'conv_b': nrm(ks[10], (DEPTH, C_CONV)) * 0.01,
        'conv_ln_g': 1.0 + 0.02 * nrm(ks[11], (DEPTH, C_CONV)),
        'conv_ln_b': 0.02 * nrm(ks[12], (DEPTH, C_CONV)),
        'w_proj_a': nrm(ks[13], (DEPTH, C_CONV, D_MODEL)) * (C_CONV ** -0.5) * BETA,
        'lambda_q1': nrm(ks[14], (DEPTH, HEAD_DIM)) * 0.1,
        'lambda_k1': nrm(ks[15], (DEPTH, HEAD_DIM)) * 0.1,
        'lambda_q2': nrm(ks[16], (DEPTH, HEAD_DIM)) * 0.1,
        'lambda_k2': nrm(ks[17], (DEPTH, HEAD_DIM)) * 0.1,
        'subln_w': 1.0 + 0.02 * nrm(ks[18], (DEPTH, V_DIM)),
        'w_proj_b': nrm(ks[19], (DEPTH, ATT_WIDTH, D_MODEL)) * (ATT_WIDTH ** -0.5) * BETA,
        'w_out': nrm(ks[20], (DEPTH, D_MODEL, D_MODEL)) * (D_MODEL ** -0.5) * BETA,
        'ln_g': 1.0 + 0.02 * nrm(ks[21], (DEPTH, D_MODEL)),
        'ln_b': 0.02 * nrm(ks[22], (DEPTH, D_MODEL)),
        'w_ple_proj': nrm(ks[23], (DEPTH, PLE_DIM, D_MODEL)) * (PLE_DIM ** -0.5),
        'w_ple_gate': nrm(ks[24], (DEPTH, D_MODEL, D_MODEL)) * (D_MODEL ** -0.5),
        'rel_bias': nrm(ks[25], (N_BUCKETS, N_HEADS)) * 0.2,
    }


def reference(x_prompt, x_sample, p_prompt, p_sample, cache_k, cache_v, state_conv, page_table,
              w_in, conv_w, conv_b, conv_ln_g, conv_ln_b, w_proj_a, lambda_q1, lambda_k1, lambda_q2, lambda_k2,
              subln_w, w_proj_b, w_out, ln_g, ln_b, w_ple_proj, w_ple_gate, rel_bias):
    b, db = x_prompt.shape[0], x_sample.shape[0]
    xp, xs = x_prompt, x_sample
    kp_l, vp_l, cp_l, ks_l, vs_l, cs_l = [], [], [], [], [], []
    for i in range(DEPTH):
        lam_init = 0.8 - 0.6 * math.exp(-0.3 * i)
        lam = _lambda(lambda_q1[i], lambda_k1[i], lambda_q2[i], lambda_k2[i], lam_init)
        wts = (w_in[i], conv_w[i], conv_b[i], conv_ln_g[i], conv_ln_b[i], w_proj_a[i], subln_w[i], lam_init,
               w_proj_b[i], w_out[i], ln_g[i], ln_b[i], w_ple_proj[i], w_ple_gate[i])
        attend_p = lambda q, k, v: _attn_prompt(q, k, v, rel_bias, lam)
        conv0 = jnp.zeros((b, CONV_WIDTH - 1, C_CONV), xp.dtype)
        xp, kp, vp, cp = _layer(xp, p_prompt[i], conv0, attend_p, *wts)
        k_past = cache_k[i][page_table].reshape(db, -1, 2 * N_HEADS, HEAD_DIM)
        v_past = cache_v[i][page_table].reshape(db, -1, N_HEADS, V_DIM)
        attend_s = lambda q, k, v: _attn_sample(q, k, v, k_past, v_past, rel_bias, lam)
        xs, kn, vn, cn = _layer(xs, p_sample[i], state_conv[i], attend_s, *wts)
        kp_l.append(kp); vp_l.append(vp); cp_l.append(cp)
        ks_l.append(kn); vs_l.append(vn); cs_l.append(cn)
    return (xp, xs, jnp.stack(kp_l), jnp.stack(vp_l), jnp.stack(cp_l), jnp.stack(ks_l), jnp.stack(vs_l), jnp.stack(cs_l))
```

```python
import functools
import math

import jax
import jax.numpy as jnp
from jax import lax
from jax.experimental import pallas as pl
from jax.experimental.pallas import tpu as pltpu

F32 = jnp.float32
BF16 = jnp.bfloat16

N_HEADS = 8
HEAD_DIM = 128
V_DIM = 2 * HEAD_DIM
CONV_WIDTH = 31
N_BUCKETS = 32
MAX_DISTANCE = 128
LN_EPS = 1e-5
NEG_INF = -1e30
DEPTH = 1
ALPHA = (2 * DEPTH) ** 0.25
LAM_INIT = 0.8 - 0.6 * math.exp(-0.3 * 0)
QK_SCALE = HEAD_DIM ** -0.5

V7X_VMEM_BYTES = 64 * 1024 * 1024
V7X_SUBLANES = 8
MIB = 1024 * 1024

PROJ_TN = 1024
PROJ_TM = 1024
CONV_TC = 256
CONV_RB = 32
CONV_HALO = 32
ATT_TQ = 512
ATT_TK = 512
DEC_PAGES = 8
FINAL_TM = 256


def _cparams(sem, vmem_mib):
    limit = min(vmem_mib * MIB, V7X_VMEM_BYTES - 6 * MIB)
    return pltpu.CompilerParams(dimension_semantics=sem, vmem_limit_bytes=limit)


def _proj_kernel(x_ref, *refs, n_w, epilogue):
    w_refs, out_refs = refs[:n_w], refs[n_w:]
    x = x_ref[...]
    accs = [jnp.dot(x, w[...], preferred_element_type=F32) for w in w_refs]
    for o_ref, o in zip(out_refs, epilogue(*accs)):
        o_ref[...] = o.astype(o_ref.dtype)


def _proj(name, x_bf, w_bf, col_starts, ncols, epilogue, out_dtypes, tm):
    m, k = x_bf.shape
    tn = PROJ_TN
    n_w = len(col_starts)
    in_specs = [pl.BlockSpec((tm, k), lambda j, i: (i, 0))]
    for c0 in col_starts:
        assert c0 % tn == 0
        in_specs.append(pl.BlockSpec((k, tn), lambda j, i, c0=c0: (0, c0 // tn + j)))
    out_specs = [pl.BlockSpec((tm, tn), lambda j, i: (i, j)) for _ in out_dtypes]
    out_shape = [jax.ShapeDtypeStruct((m, ncols), dt) for dt in out_dtypes]
    vmem = 2 * (tm * k * 2 + n_w * k * tn * 2 + sum(tm * tn * jnp.dtype(d).itemsize for d in out_dtypes))
    vmem += n_w * tm * tn * 4
    return pl.pallas_call(
        functools.partial(_proj_kernel, n_w=n_w, epilogue=epilogue),
        grid=(ncols // tn, m // tm),
        in_specs=in_specs, out_specs=out_specs, out_shape=out_shape,
        compiler_params=_cparams(("arbitrary", "arbitrary"), vmem // MIB + 6),
        name=f"proj_{name}_m{m}",
    )(x_bf, *([w_bf] * n_w))


def _ep_glu(ua, ug):
    return (ua * jax.nn.sigmoid(ug),)


def _ep_silu(z):
    return (jax.nn.silu(z),)


def _ep_sigmoid(z):
    return (jax.nn.sigmoid(z),)


def _ep_copy1(z):
    return (z,)


def _ep_copy2(z):
    return (z, z)


def _in_projection(x_bf, w_bf, tm, c_conv, qk_w, att_w, d_model):
    o = 0
    (g,) = _proj("glu", x_bf, w_bf, (o, o + c_conv), c_conv, _ep_glu, (F32,), tm)
    o += 2 * c_conv
    (sza,) = _proj("za", x_bf, w_bf, (o,), c_conv, _ep_silu, (F32,), tm)
    o += c_conv
    (q_bf,) = _proj("q", x_bf, w_bf, (o,), qk_w, _ep_copy1, (BF16,), tm)
    o += qk_w
    k, k_bf = _proj("k", x_bf, w_bf, (o,), qk_w, _ep_copy2, (F32, BF16), tm)
    o += qk_w
    v, v_bf = _proj("v", x_bf, w_bf, (o,), att_w, _ep_copy2, (F32, BF16), tm)
    o += att_w
    (szb,) = _proj("zb", x_bf, w_bf, (o,), att_w, _ep_silu, (F32,), tm)
    o += att_w
    (sgab,) = _proj("gab", x_bf, w_bf, (o,), 2 * d_model, _ep_sigmoid, (F32,), tm)
    return g, sza, q_bf, k, k_bf, v, v_bf, szb, sgab


def _ln_swish(acc, lng, lnb):
    mu = jnp.mean(acc, axis=-1, keepdims=True)
    xc = acc - mu
    var = jnp.mean(xc * xc, axis=-1, keepdims=True)
    y = xc * lax.rsqrt(var + LN_EPS) * lng + lnb
    return y * jax.nn.sigmoid(y)


def _conv_prompt_kernel(g_ref, halo_ref, prev_ref, cw_ref, cb_ref, lng_ref, lnb_ref, sza_ref, sga_ref, wpa_ref,
                        out_ref, xs_ref, hs_ref, *, tc, rb):
    i = pl.program_id(1)
    n = tc + CONV_HALO
    xs_ref[0, 0:CONV_HALO, :] = jnp.where(i == 0, prev_ref[0], halo_ref[0])
    xs_ref[0, CONV_HALO:n, :] = g_ref[0]
    for r in range(1, V7X_SUBLANES):
        xs_ref[r, 0:n - V7X_SUBLANES, :] = xs_ref[0, r:r + n - V7X_SUBLANES, :]

    first = CONV_HALO - (CONV_WIDTH - 1)

    def body(tb, carry):
        t0 = pl.multiple_of(tb * rb, rb)
        acc = jnp.broadcast_to(cb_ref[...], (rb, cb_ref.shape[-1]))
        for j in range(CONV_WIDTH):
            r, a = (first + j) % V7X_SUBLANES, (first + j) // V7X_SUBLANES
            acc = acc + cw_ref[j:j + 1, :] * xs_ref[r, pl.ds(t0 + V7X_SUBLANES * a, rb), :]
        h = _ln_swish(acc, lng_ref[...], lnb_ref[...]) * sza_ref[0, pl.ds(t0, rb), :]
        hs_ref[pl.ds(t0, rb), :] = h.astype(BF16)
        return carry

    lax.fori_loop(0, tc // rb, body, 0)
    a_proj = jnp.dot(hs_ref[...], wpa_ref[...], preferred_element_type=F32)
    out_ref[0] = sga_ref[0] * a_proj


def _conv_prompt(g, prev, conv_w, conv_b, lng, lnb, sza, sgab, wpa_bf):
    b, t, c = g.shape
    d = wpa_bf.shape[1]
    tc, rb = CONV_TC, CONV_RB
    hpt = tc // CONV_HALO
    vmem = 2 * (2 * tc * c * 4 + tc * d * 4 * 2 + c * d * 2) + 8 * (tc + CONV_HALO) * c * 4 + tc * c * 2 + tc * d * 4
    return pl.pallas_call(
        functools.partial(_conv_prompt_kernel, tc=tc, rb=rb),
        grid=(b, t // tc),
        in_specs=[
            pl.BlockSpec((1, tc, c), lambda bi, i: (bi, i, 0)),
            pl.BlockSpec((1, CONV_HALO, c), lambda bi, i: (bi, jnp.maximum(i * hpt - 1, 0), 0)),
            pl.BlockSpec((1, CONV_HALO, c), lambda bi, i: (bi, 0, 0)),
            pl.BlockSpec((CONV_WIDTH, c), lambda bi, i: (0, 0)),
            pl.BlockSpec((1, c), lambda bi, i: (0, 0)),
            pl.BlockSpec((1, c), lambda bi, i: (0, 0)),
            pl.BlockSpec((1, c), lambda bi, i: (0, 0)),
            pl.BlockSpec((1, tc, c), lambda bi, i: (bi, i, 0)),
            pl.BlockSpec((1, tc, d), lambda bi, i: (bi, i, 0)),
            pl.BlockSpec((c, d), lambda bi, i: (0, 0)),
        ],
        out_specs=pl.BlockSpec((1, tc, d), lambda bi, i: (bi, i, 0)),
        out_shape=jax.ShapeDtypeStruct((b, t, d), F32),
        scratch_shapes=[pltpu.VMEM((V7X_SUBLANES, tc + CONV_HALO, c), F32), pltpu.VMEM((tc, c), BF16)],
        compiler_params=_cparams(("arbitrary", "arbitrary"), vmem // MIB + 6),
        name="conv_prompt",
    )(g, g, prev, conv_w, conv_b, lng, lnb, sza, sgab, wpa_bf)


def _conv_sample_kernel(xc_ref, cw_ref, cb_ref, lng_ref, lnb_ref, sza_ref, sga_ref, wpa_ref, out_ref, hs_ref, *, t_new):
    nb = xc_ref.shape[1]
    for t in range(t_new):
        acc = jnp.broadcast_to(cb_ref[...], (nb, cb_ref.shape[-1]))
        for j in range(CONV_WIDTH):
            acc = acc + cw_ref[j:j + 1, :] * xc_ref[t + j]
        h = _ln_swish(acc, lng_ref[...], lnb_ref[...]) * sza_ref[t * nb:(t + 1) * nb, :]
        hs_ref[t * nb:(t + 1) * nb, :] = h.astype(BF16)
    out_ref[...] = sga_ref[...] * jnp.dot(hs_ref[...], wpa_ref[...], preferred_element_type=F32)


def _conv_sample(xc_tm, conv_w, conv_b, lng, lnb, sza_tm, sga_tm, wpa_bf, t_new):
    _, nb, c = xc_tm.shape
    d = wpa_bf.shape[1]
    return pl.pallas_call(
        functools.partial(_conv_sample_kernel, t_new=t_new),
        out_shape=jax.ShapeDtypeStruct((t_new * nb, d), F32),
        scratch_shapes=[pltpu.VMEM((t_new * nb, c), BF16)],
        compiler_params=pltpu.CompilerParams(vmem_limit_bytes=32 * MIB),
        name="conv_sample",
    )(xc_tm, conv_w, conv_b, lng, lnb, sza_tm, sga_tm, wpa_bf)


def _lambda_value(lq1_ref, lk1_ref, lq2_ref, lk2_ref):
    s1 = jnp.sum(lq1_ref[...] * lk1_ref[...], axis=-1, keepdims=True)
    s2 = jnp.sum(lq2_ref[...] * lk2_ref[...], axis=-1, keepdims=True)
    return jnp.exp(s1) - jnp.exp(s2) + LAM_INIT


def _subln_gate(o, sub, szb):
    r = o * lax.rsqrt(jnp.mean(o * o, axis=-1, keepdims=True) + LN_EPS) * sub * (1.0 - LAM_INIT)
    return (r * szb).astype(BF16)


def _bias_by_distance(rel_bias, n_max):
    n = jnp.arange(n_max)
    max_exact = N_BUCKETS // 2
    large = max_exact + (jnp.log(jnp.maximum(n, 1).astype(F32) / max_exact)
                         / math.log(MAX_DISTANCE / max_exact) * (N_BUCKETS - max_exact)).astype(jnp.int32)
    large = jnp.minimum(large, N_BUCKETS - 1)
    bucket = jnp.where(n < max_exact, n, large)
    return jnp.transpose(rel_bias[bucket].astype(F32), (1, 0))


def _attn_prompt_kernel(cfar_ref, q_ref, k_ref, v_ref, tz_ref, szb_ref, sub_ref, lq1_ref, lk1_ref, lq2_ref, lk2_ref,
                        out_ref, m_ref, l_ref, acc_ref, *, tq, tk):
    h = pl.program_id(1)
    qi = pl.program_id(2)
    q = q_ref[0]
    qs = (q[:, :HEAD_DIM], q[:, HEAD_DIM:])
    m_ref[...] = jnp.full(m_ref.shape, NEG_INF, F32)
    l_ref[...] = jnp.zeros(l_ref.shape, F32)
    acc_ref[...] = jnp.zeros(acc_ref.shape, F32)

    def step(j, bias, mask):
        off = pl.multiple_of(j * tk, tk)
        kj = k_ref[0, pl.ds(off, tk), :]
        vj = v_ref[0, pl.ds(off, tk), :]
        for c in range(2):
            s = lax.dot_general(qs[c], kj[:, c * HEAD_DIM:(c + 1) * HEAD_DIM], (((1,), (1,)), ((), ())),
                                preferred_element_type=F32)
            s = s * QK_SCALE + bias
            if mask is not None:
                s = jnp.where(mask, s, NEG_INF)
            m_old = m_ref[c]
            m_new = jnp.maximum(m_old, jnp.max(s, axis=-1, keepdims=True))
            alpha = jnp.exp(m_old - m_new)
            p = jnp.exp(s - m_new)
            l_ref[c] = alpha * l_ref[c] + jnp.sum(p, axis=-1, keepdims=True)
            acc_ref[c] = alpha * acc_ref[c] + jnp.dot(p.astype(BF16), vj, preferred_element_type=F32)
            m_ref[c] = m_new

    cfar = cfar_ref[h]

    def far_body(j, carry):
        step(j, cfar, None)
        return carry

    lax.fori_loop(0, jnp.maximum(qi - 1, 0), far_body, 0)

    @pl.when(qi >= 1)
    def _():
        step(qi - 1, tz_ref[0, 1], None)

    row = lax.broadcasted_iota(jnp.int32, (tq, tk), 0)
    col = lax.broadcasted_iota(jnp.int32, (tq, tk), 1)
    step(qi, tz_ref[0, 0], col <= row)

    lam = _lambda_value(lq1_ref, lk1_ref, lq2_ref, lk2_ref)
    o = acc_ref[0] / l_ref[0] - lam * (acc_ref[1] / l_ref[1])
    out_ref[0] = _subln_gate(o, sub_ref[...], szb_ref[0])


def _attn_prompt(q_bf, k_bf, v_bf, szb, bd, sub, lq1, lk1, lq2, lk2):
    b, t, _ = q_bf.shape
    tq, tk = ATT_TQ, ATT_TK
    assert tq == tk and tq >= MAX_DISTANCE and t % tq == 0
    dist = (jnp.arange(2)[:, None, None] * tk + jnp.arange(tq)[None, :, None] - jnp.arange(tk)[None, None, :])
    tz = bd[:, jnp.maximum(dist, 0)]
    cfar = bd[:, 2 * tk]
    smem = pl.BlockSpec(memory_space=pltpu.SMEM)
    vec = pl.BlockSpec((1, HEAD_DIM), lambda bi, h, i: (0, 0))
    vmem = 2 * (2 * t * V_DIM * 2 + 2 * tq * tk * 4 + tq * V_DIM * (2 + 4 + 2)) + 3 * tq * V_DIM * 4 + 8 * tq * tk * 4
    return pl.pallas_call(
        functools.partial(_attn_prompt_kernel, tq=tq, tk=tk),
        grid=(b, N_HEADS, t // tq),
        in_specs=[
            smem,
            pl.BlockSpec((1, tq, V_DIM), lambda bi, h, i: (bi, i, h)),
            pl.BlockSpec((1, t, V_DIM), lambda bi, h, i: (bi, 0, h)),
            pl.BlockSpec((1, t, V_DIM), lambda bi, h, i: (bi, 0, h)),
            pl.BlockSpec((1, 2, tq, tk), lambda bi, h, i: (h, 0, 0, 0)),
            pl.BlockSpec((1, tq, V_DIM), lambda bi, h, i: (bi, i, h)),
            pl.BlockSpec((1, V_DIM), lambda bi, h, i: (0, 0)),
            vec, vec, vec, vec,
        ],
        out_specs=pl.BlockSpec((1, tq, V_DIM), lambda bi, h, i: (bi, i, h)),
        out_shape=jax.ShapeDtypeStruct((b, t, N_HEADS * V_DIM), BF16),
        scratch_shapes=[pltpu.VMEM((2, tq, 1), F32), pltpu.VMEM((2, tq, 1), F32), pltpu.VMEM((2, tq, V_DIM), F32)],
        compiler_params=_cparams(("arbitrary", "arbitrary", "arbitrary"), vmem // MIB + 6),
        name="attn_prompt",
    )(cfar, q_bf, k_bf, v_bf, tz, szb, sub, lq1, lk1, lq2, lk2)


def _attn_decode_kernel(pt_ref, wq_ref, bias_ref, knew_ref, vnew_ref, bnew_ref, szb_ref, sub_ref,
                        lq1_ref, lk1_ref, lq2_ref, lk2_ref, *refs, n_pages, t_new):
    del pt_ref
    k_refs, v_refs = refs[:n_pages], refs[n_pages:2 * n_pages]
    out_ref, m_ref, l_ref, acc_ref = refs[2 * n_pages:]
    s_idx = pl.program_id(1)
    page = k_refs[0].shape[1]
    wq = wq_ref[0]
    nt = (((1,), (1,)), ((), ()))

    @pl.when(s_idx == 0)
    def _():
        m_ref[...] = jnp.full(m_ref.shape, NEG_INF, F32)
        l_ref[...] = jnp.zeros(l_ref.shape, F32)
        acc_ref[...] = jnp.zeros(acc_ref.shape, F32)

    def update(s, pv_fn):
        m_old = m_ref[...]
        m_new = jnp.maximum(m_old, jnp.max(s, axis=-1, keepdims=True))
        alpha = jnp.exp(m_old - m_new)
        p = jnp.exp(s - m_new)
        l_ref[...] = alpha * l_ref[...] + jnp.sum(p, axis=-1, keepdims=True)
        acc_ref[...] = alpha * acc_ref[...] + pv_fn(p.astype(BF16))
        m_ref[...] = m_new

    s_parts = [lax.dot_general(wq, k_refs[i][0].astype(BF16), nt, preferred_element_type=F32) for i in range(n_pages)]
    s_past = jnp.concatenate(s_parts, axis=-1) * QK_SCALE + bias_ref[...]

    def pv_past(p):
        out = jnp.dot(p[:, 0:page], v_refs[0][0].astype(BF16), preferred_element_type=F32)
        for i in range(1, n_pages):
            out = out + jnp.dot(p[:, i * page:(i + 1) * page], v_refs[i][0].astype(BF16), preferred_element_type=F32)
        return out

    update(s_past, pv_past)

    @pl.when(s_idx == pl.num_programs(1) - 1)
    def _():
        s_new = lax.dot_general(wq, knew_ref[0], nt, preferred_element_type=F32) * QK_SCALE + bnew_ref[...]
        row = lax.broadcasted_iota(jnp.int32, s_new.shape, 0)
        col = lax.broadcasted_iota(jnp.int32, s_new.shape, 1)
        s_new = jnp.where(col <= row % t_new, s_new, NEG_INF)
        update(s_new, lambda p: jnp.dot(p, vnew_ref[0], preferred_element_type=F32))

        lam = _lambda_value(lq1_ref, lk1_ref, lq2_ref, lk2_ref)
        rows_per_head = 2 * t_new
        for h in range(N_HEADS):
            r0 = h * rows_per_head
            blk = acc_ref[r0:r0 + rows_per_head, h * V_DIM:(h + 1) * V_DIM] / l_ref[r0:r0 + rows_per_head, :]
            o = blk[0:t_new] - lam * blk[t_new:rows_per_head]
            out_ref[0, :, h * V_DIM:(h + 1) * V_DIM] = _subln_gate(
                o, sub_ref[...], szb_ref[0, :, h * V_DIM:(h + 1) * V_DIM])


def _attn_decode(page_table, cache_k3, cache_v3, wq, bias_past, k_new, v_new, bias_new, szb, sub, lq1, lk1, lq2, lk2):
    nb, n_pt = page_table.shape
    _, page, width = cache_k3.shape
    t_new = szb.shape[1]
    rows = wq.shape[1]
    npg = DEC_PAGES
    assert n_pt % npg == 0
    new_pad = k_new.shape[1]

    def page_spec(i):
        return pl.BlockSpec((1, page, width), lambda b, s, pt, i=i: (pt[b, s * npg + i], 0, 0))

    vec = pl.BlockSpec((1, HEAD_DIM), lambda b, s, pt: (0, 0))
    in_specs = [
        pl.BlockSpec((1, rows, width), lambda b, s, pt: (b, 0, 0)),
        pl.BlockSpec((rows, npg * page), lambda b, s, pt: (0, s)),
        pl.BlockSpec((1, new_pad, width), lambda b, s, pt: (b, 0, 0)),
        pl.BlockSpec((1, new_pad, width), lambda b, s, pt: (b, 0, 0)),
        pl.BlockSpec((rows, new_pad), lambda b, s, pt: (0, 0)),
        pl.BlockSpec((1, t_new, width), lambda b, s, pt: (b, 0, 0)),
        pl.BlockSpec((1, V_DIM), lambda b, s, pt: (0, 0)),
        vec, vec, vec, vec,
    ] + [page_spec(i) for i in range(npg)] + [page_spec(i) for i in range(npg)]
    vmem = 2 * 2 * npg * page * width * 4 + 6 * rows * width * 4 + 4 * MIB
    return pl.pallas_call(
        functools.partial(_attn_decode_kernel, n_pages=npg, t_new=t_new),
        grid_spec=pltpu.PrefetchScalarGridSpec(
            num_scalar_prefetch=1,
            grid=(nb, n_pt // npg),
            in_specs=in_specs,
            out_specs=pl.BlockSpec((1, t_new, width), lambda b, s, pt: (b, 0, 0)),
            scratch_shapes=[pltpu.VMEM((rows, 1), F32), pltpu.VMEM((rows, 1), F32), pltpu.VMEM((rows, width), F32)],
        ),
        out_shape=jax.ShapeDtypeStruct((nb, t_new, width), BF16),
        compiler_params=_cparams(("arbitrary", "arbitrary"), vmem // MIB + 6),
        name="attn_decode",
    )(page_table, wq, bias_past, k_new, v_new, bias_new, szb, sub, lq1, lk1, lq2, lk2,
      *([cache_k3] * npg), *([cache_v3] * npg))


def _final_kernel(hb_ref, ma_ref, sgb_ref, x_ref, p_ref, wpb_ref, wout_ref, wgate_ref, wple_ref, lng_ref, lnb_ref, y_ref):
    b_proj = jnp.dot(hb_ref[...], wpb_ref[...], preferred_element_type=F32)
    merged = ma_ref[...] + sgb_ref[...] * b_proj
    z = ALPHA * x_ref[...] + jnp.dot(merged.astype(BF16), wout_ref[...], preferred_element_type=F32)
    mu = jnp.mean(z, axis=-1, keepdims=True)
    zc = z - mu
    var = jnp.mean(zc * zc, axis=-1, keepdims=True)
    h = zc * lax.rsqrt(var + LN_EPS) * lng_ref[...] + lnb_ref[...]
    gate = jax.nn.sigmoid(jnp.dot(h.astype(BF16), wgate_ref[...], preferred_element_type=F32))
    pe = jnp.dot(p_ref[...].astype(BF16), wple_ref[...], preferred_element_type=F32)
    y_ref[...] = h + gate * pe


def _final(hb, ma, sgab, x, p, wpb_bf, wout_bf, wgate_bf, wple_bf, lng, lnb, tm):
    m, d = x.shape
    pd = p.shape[1]
    nblk_d = 1
    row = lambda w: pl.BlockSpec((tm, w), lambda i: (i, 0))
    const = lambda r, c: pl.BlockSpec((r, c), lambda i: (0, 0), pipeline_mode=pl.Buffered(1))
    vmem = 2 * tm * (d * 2 + 4 * d * 4 + pd * 4) + (3 * d * d + pd * d) * 2 + 6 * tm * d * 4
    return pl.pallas_call(
        _final_kernel,
        grid=(m // tm,),
        in_specs=[row(d), row(d), pl.BlockSpec((tm, d), lambda i: (i, nblk_d)), row(d), row(pd),
                  const(d, d), const(d, d), const(d, d), const(pd, d), const(1, d), const(1, d)],
        out_specs=row(d),
        out_shape=jax.ShapeDtypeStruct((m, d), F32),
        compiler_params=_cparams(("arbitrary",), vmem // MIB + 6),
        name=f"final_m{m}",
    )(hb, ma, sgab, x, p, wpb_bf, wout_bf, wgate_bf, wple_bf, lng, lnb)


def kernel(x_prompt, x_sample, p_prompt, p_sample, cache_k, cache_v, state_conv, page_table, w_in, conv_w, conv_b, conv_ln_g, conv_ln_b, w_proj_a, lambda_q1, lambda_k1, lambda_q2, lambda_k2, subln_w, w_proj_b, w_out, ln_g, ln_b, w_ple_proj, w_ple_gate, rel_bias):
    assert w_in.shape[0] == DEPTH
    b, t, d = x_prompt.shape
    nb, tn, _ = x_sample.shape
    c = conv_w.shape[-1]
    qk_w = 2 * N_HEADS * HEAD_DIM
    att_w = N_HEADS * V_DIM
    pd = p_prompt.shape[-1]
    n_phys, page = cache_k.shape[1], cache_k.shape[2]
    past = page_table.shape[1] * page

    w_bf = w_in[0].astype(BF16)
    wpa_bf = w_proj_a[0].astype(BF16)
    wpb_bf = w_proj_b[0].astype(BF16)
    wout_bf = w_out[0].astype(BF16)
    wgate_bf = w_ple_gate[0].astype(BF16)
    wple_bf = w_ple_proj[0].astype(BF16)
    cw, cb = conv_w[0], conv_b[0].reshape(1, c)
    clng, clnb = conv_ln_g[0].reshape(1, c), conv_ln_b[0].reshape(1, c)
    lng, lnb = ln_g[0].reshape(1, d), ln_b[0].reshape(1, d)
    sub = subln_w[0].reshape(1, V_DIM)
    lq1, lk1, lq2, lk2 = (v[0].reshape(1, HEAD_DIM) for v in (lambda_q1, lambda_k1, lambda_q2, lambda_k2))
    bd = _bias_by_distance(rel_bias, past + tn)

    xp = x_prompt.reshape(b * t, d)
    g, sza, q_bf, k_p, k_bf, v_p, v_bf, szb, sgab = _in_projection(xp.astype(BF16), w_bf, PROJ_TM, c, qk_w, att_w, d)
    g3 = g.reshape(b, t, c)
    ma = _conv_prompt(g3, jnp.zeros((b, CONV_HALO, c), F32), cw, cb, clng, clnb, sza.reshape(b, t, c),
                      sgab.reshape(b, t, 2 * d), wpa_bf)
    hb = _attn_prompt(q_bf.reshape(b, t, qk_w), k_bf.reshape(b, t, qk_w), v_bf.reshape(b, t, att_w),
                      szb.reshape(b, t, att_w), bd, sub, lq1, lk1, lq2, lk2)
    y_p = _final(hb.reshape(b * t, att_w), ma.reshape(b * t, d), sgab, xp, p_prompt[0].reshape(b * t, pd),
                 wpb_bf, wout_bf, wgate_bf, wple_bf, lng, lnb, FINAL_TM)

    ms = nb * tn
    xs = x_sample.reshape(ms, d)
    g_s, sza_s, q_s, k_s, k_s_bf, v_s, v_s_bf, szb_s, sgab_s = _in_projection(xs.astype(BF16), w_bf, ms, c, qk_w, att_w, d)
    tm_major = lambda a: jnp.transpose(a.reshape(nb, tn, a.shape[-1]), (1, 0, 2))
    xc_tm = jnp.concatenate([jnp.transpose(state_conv[0], (1, 0, 2)), tm_major(g_s)], axis=0)
    ma_s_tm = _conv_sample(xc_tm, cw, cb, clng, clnb, tm_major(sza_s).reshape(ms, c),
                           tm_major(sgab_s[:, :d]).reshape(ms, d), wpa_bf, tn)
    ma_s = jnp.transpose(ma_s_tm.reshape(tn, nb, d), (1, 0, 2)).reshape(ms, d)

    n_maps = 2 * N_HEADS
    wq = jnp.einsum('bqgd,gh->bgqhd', q_s.reshape(nb, tn, n_maps, HEAD_DIM), jnp.eye(n_maps, dtype=BF16))
    wq = wq.reshape(nb, n_maps * tn, qk_w)
    rows = n_maps * tn
    row_head = jnp.arange(rows) // (2 * tn)
    row_q = jnp.arange(rows) % tn
    bias_past = bd[row_head[:, None], past + row_q[:, None] - jnp.arange(past)[None, :]]
    new_pad = 2 * V7X_SUBLANES
    bias_new = bd[row_head[:, None], jnp.maximum(row_q[:, None] - jnp.arange(new_pad)[None, :], 0)]
    pad_new = lambda a: jnp.pad(a.reshape(nb, tn, a.shape[-1]), ((0, 0), (0, new_pad - tn), (0, 0)))
    hb_s = _attn_decode(page_table, cache_k[0].reshape(n_phys, page, qk_w), cache_v[0].reshape(n_phys, page, att_w),
                        wq, bias_past, pad_new(k_s_bf), pad_new(v_s_bf), bias_new,
                        szb_s.reshape(nb, tn, att_w), sub, lq1, lk1, lq2, lk2)
    y_s = _final(hb_s.reshape(ms, att_w), ma_s, sgab_s, xs, p_sample[0].reshape(ms, pd),
                 wpb_bf, wout_bf, wgate_bf, wple_bf, lng, lnb, ms)

    conv_prompt = g3[:, t - (CONV_WIDTH - 1):, :]
    conv_sample = jnp.concatenate([state_conv[0][:, tn:, :], g_s.reshape(nb, tn, c)], axis=1)
    return (y_p.reshape(b, t, d), y_s.reshape(nb, tn, d),
            k_p.reshape(1, b, t, n_maps, HEAD_DIM), v_p.reshape(1, b, t, N_HEADS, V_DIM), conv_prompt[None],
            k_s.reshape(1, nb, tn, n_maps, HEAD_DIM), v_s.reshape(1, nb, tn, N_HEADS, V_DIM), conv_sample[None])
```

```python
import functools
import math

import numpy as np
import jax
import jax.numpy as jnp
from jax import lax
from jax.experimental import pallas as pl
from jax.experimental.pallas import tpu as pltpu

F32 = jnp.float32
BF16 = jnp.bfloat16

N_HEADS = 8
HEAD_DIM = 128
V_DIM = 2 * HEAD_DIM
CONV_WIDTH = 31
N_BUCKETS = 32
MAX_DISTANCE = 128
LN_EPS = 1e-5
NEG_INF = -1e30
DEPTH = 1
ALPHA = (2 * DEPTH) ** 0.25
LAM_INIT = 0.8 - 0.6 * math.exp(-0.3 * 0)
QK_SCALE = HEAD_DIM ** -0.5
LOG2E = math.log2(math.e)

V7X_VMEM_BYTES = 64 * 1024 * 1024
V7X_SUBLANES = 8
LANES = 128
MIB = 1024 * 1024

PROJ_TN = 1024
PROJ_TM = 1024
CONV_TC = 256
CONV_RB = 32
CONV_HALO = 32
ATT_TQ = 512
ATT_TK = 512
DEC_PAGES = 8
FINAL_TM = 256


def _cparams(sem, vmem_mib):
    limit = min(vmem_mib * MIB, V7X_VMEM_BYTES - 6 * MIB)
    return pltpu.CompilerParams(dimension_semantics=sem, vmem_limit_bytes=limit)


def _proj_kernel(x_ref, *refs, n_w, epilogue):
    w_refs, out_refs = refs[:n_w], refs[n_w:]
    x = x_ref[...]
    accs = [jnp.dot(x, w[...], preferred_element_type=F32) for w in w_refs]
    for o_ref, o in zip(out_refs, epilogue(*accs)):
        o_ref[...] = o.astype(o_ref.dtype)


def _proj(name, x_bf, w_bf, col_starts, ncols, epilogue, out_dtypes, tm):
    m, k = x_bf.shape
    tn = PROJ_TN
    n_w = len(col_starts)
    in_specs = [pl.BlockSpec((tm, k), lambda j, i: (i, 0))]
    for c0 in col_starts:
        assert c0 % tn == 0
        in_specs.append(pl.BlockSpec((k, tn), lambda j, i, c0=c0: (0, c0 // tn + j)))
    out_specs = [pl.BlockSpec((tm, tn), lambda j, i: (i, j)) for _ in out_dtypes]
    out_shape = [jax.ShapeDtypeStruct((m, ncols), dt) for dt in out_dtypes]
    vmem = 2 * (tm * k * 2 + n_w * k * tn * 2 + sum(tm * tn * jnp.dtype(d).itemsize for d in out_dtypes))
    vmem += n_w * tm * tn * 4
    return pl.pallas_call(
        functools.partial(_proj_kernel, n_w=n_w, epilogue=epilogue),
        grid=(ncols // tn, m // tm),
        in_specs=in_specs, out_specs=out_specs, out_shape=out_shape,
        compiler_params=_cparams(("arbitrary", "arbitrary"), vmem // MIB + 6),
        name=f"proj_{name}_m{m}",
    )(x_bf, *([w_bf] * n_w))


def _ep_glu(ua, ug):
    return (ua * jax.nn.sigmoid(ug),)


def _ep_silu(z):
    return (jax.nn.silu(z),)


def _ep_sigmoid(z):
    return (jax.nn.sigmoid(z),)


def _ep_query(z):
    return (z * (QK_SCALE * LOG2E),)


def _ep_copy2(z):
    return (z, z)


def _in_projection(x_bf, w_bf, tm, c_conv, qk_w, att_w, d_model):
    o = 0
    (g,) = _proj("glu", x_bf, w_bf, (o, o + c_conv), c_conv, _ep_glu, (F32,), tm)
    o += 2 * c_conv
    (sza,) = _proj("za", x_bf, w_bf, (o,), c_conv, _ep_silu, (F32,), tm)
    o += c_conv
    (q_bf,) = _proj("q", x_bf, w_bf, (o,), qk_w, _ep_query, (BF16,), tm)
    o += qk_w
    k, k_bf = _proj("k", x_bf, w_bf, (o,), qk_w, _ep_copy2, (F32, BF16), tm)
    o += qk_w
    v, v_bf = _proj("v", x_bf, w_bf, (o,), att_w, _ep_copy2, (F32, BF16), tm)
    o += att_w
    (szb,) = _proj("zb", x_bf, w_bf, (o,), att_w, _ep_silu, (F32,), tm)
    o += att_w
    (sgab,) = _proj("gab", x_bf, w_bf, (o,), 2 * d_model, _ep_sigmoid, (F32,), tm)
    return g, sza, q_bf, k, k_bf, v, v_bf, szb, sgab


def _ln_swish(acc, lng, lnb):
    mu = jnp.mean(acc, axis=-1, keepdims=True)
    xc = acc - mu
    var = jnp.mean(xc * xc, axis=-1, keepdims=True)
    y = xc * lax.rsqrt(var + LN_EPS) * lng + lnb
    return y * jax.nn.sigmoid(y)


def _conv_prompt_kernel(g_ref, halo_ref, prev_ref, cw_ref, cb_ref, lng_ref, lnb_ref, sza_ref, sga_ref, wpa_ref,
                        out_ref, xs_ref, hs_ref, *, tc, rb):
    i = pl.program_id(1)
    n = tc + CONV_HALO
    xs_ref[0, 0:CONV_HALO, :] = jnp.where(i == 0, prev_ref[0], halo_ref[0])
    xs_ref[0, CONV_HALO:n, :] = g_ref[0]
    for r in range(1, V7X_SUBLANES):
        xs_ref[r, 0:n - V7X_SUBLANES, :] = xs_ref[0, r:r + n - V7X_SUBLANES, :]

    first = CONV_HALO - (CONV_WIDTH - 1)

    def body(tb, carry):
        t0 = pl.multiple_of(tb * rb, rb)
        acc = jnp.broadcast_to(cb_ref[...], (rb, cb_ref.shape[-1]))
        for j in range(CONV_WIDTH):
            r, a = (first + j) % V7X_SUBLANES, (first + j) // V7X_SUBLANES
            acc = acc + cw_ref[j:j + 1, :] * xs_ref[r, pl.ds(t0 + V7X_SUBLANES * a, rb), :]
        h = _ln_swish(acc, lng_ref[...], lnb_ref[...]) * sza_ref[0, pl.ds(t0, rb), :]
        hs_ref[pl.ds(t0, rb), :] = h.astype(BF16)
        return carry

    lax.fori_loop(0, tc // rb, body, 0)
    a_proj = jnp.dot(hs_ref[...], wpa_ref[...], preferred_element_type=F32)
    out_ref[0] = sga_ref[0] * a_proj


def _conv_prompt(g, prev, conv_w, conv_b, lng, lnb, sza, sgab, wpa_bf):
    b, t, c = g.shape
    d = wpa_bf.shape[1]
    tc, rb = CONV_TC, CONV_RB
    hpt = tc // CONV_HALO
    vmem = 2 * (2 * tc * c * 4 + tc * d * 4 * 2 + c * d * 2) + 8 * (tc + CONV_HALO) * c * 4 + tc * c * 2 + tc * d * 4
    return pl.pallas_call(
        functools.partial(_conv_prompt_kernel, tc=tc, rb=rb),
        grid=(b, t // tc),
        in_specs=[
            pl.BlockSpec((1, tc, c), lambda bi, i: (bi, i, 0)),
            pl.BlockSpec((1, CONV_HALO, c), lambda bi, i: (bi, jnp.maximum(i * hpt - 1, 0), 0)),
            pl.BlockSpec((1, CONV_HALO, c), lambda bi, i: (bi, 0, 0)),
            pl.BlockSpec((CONV_WIDTH, c), lambda bi, i: (0, 0)),
            pl.BlockSpec((1, c), lambda bi, i: (0, 0)),
            pl.BlockSpec((1, c), lambda bi, i: (0, 0)),
            pl.BlockSpec((1, c), lambda bi, i: (0, 0)),
            pl.BlockSpec((1, tc, c), lambda bi, i: (bi, i, 0)),
            pl.BlockSpec((1, tc, d), lambda bi, i: (bi, i, 0)),
            pl.BlockSpec((c, d), lambda bi, i: (0, 0)),
        ],
        out_specs=pl.BlockSpec((1, tc, d), lambda bi, i: (bi, i, 0)),
        out_shape=jax.ShapeDtypeStruct((b, t, d), F32),
        scratch_shapes=[pltpu.VMEM((V7X_SUBLANES, tc + CONV_HALO, c), F32), pltpu.VMEM((tc, c), BF16)],
        compiler_params=_cparams(("arbitrary", "arbitrary"), vmem // MIB + 6),
        name="conv_prompt",
    )(g, g, prev, conv_w, conv_b, lng, lnb, sza, sgab, wpa_bf)


def _conv_sample_kernel(xc_ref, cw_ref, cb_ref, lng_ref, lnb_ref, sza_ref, sga_ref, wpa_ref, out_ref, hs_ref, *, t_new):
    nb = xc_ref.shape[1]
    for t in range(t_new):
        acc = jnp.broadcast_to(cb_ref[...], (nb, cb_ref.shape[-1]))
        for j in range(CONV_WIDTH):
            acc = acc + cw_ref[j:j + 1, :] * xc_ref[t + j]
        h = _ln_swish(acc, lng_ref[...], lnb_ref[...]) * sza_ref[t * nb:(t + 1) * nb, :]
        hs_ref[t * nb:(t + 1) * nb, :] = h.astype(BF16)
    out_ref[...] = sga_ref[...] * jnp.dot(hs_ref[...], wpa_ref[...], preferred_element_type=F32)


def _conv_sample(xc_tm, conv_w, conv_b, lng, lnb, sza_tm, sga_tm, wpa_bf, t_new):
    _, nb, c = xc_tm.shape
    d = wpa_bf.shape[1]
    return pl.pallas_call(
        functools.partial(_conv_sample_kernel, t_new=t_new),
        out_shape=jax.ShapeDtypeStruct((t_new * nb, d), F32),
        scratch_shapes=[pltpu.VMEM((t_new * nb, c), BF16)],
        compiler_params=pltpu.CompilerParams(vmem_limit_bytes=32 * MIB),
        name="conv_sample",
    )(xc_tm, conv_w, conv_b, lng, lnb, sza_tm, sga_tm, wpa_bf)


def _lambda_value(lq1_ref, lk1_ref, lq2_ref, lk2_ref):
    s1 = jnp.sum(lq1_ref[...] * lk1_ref[...], axis=-1, keepdims=True)
    s2 = jnp.sum(lq2_ref[...] * lk2_ref[...], axis=-1, keepdims=True)
    return jnp.exp(s1) - jnp.exp(s2) + LAM_INIT


def _subln_gate(o, sub, szb):
    r = o * lax.rsqrt(jnp.mean(o * o, axis=-1, keepdims=True) + LN_EPS) * sub * (1.0 - LAM_INIT)
    return (r * szb).astype(BF16)


def _lane_blocks(x):
    return [x[:, i * LANES:(i + 1) * LANES] for i in range(x.shape[1] // LANES)]


def _block_max(s):
    return functools.reduce(jnp.maximum, _lane_blocks(s))


def _softmax_update(s, bmax, m_ref, l_ref, acc_ref, pv_fn):
    m_old = m_ref[...]
    m_new = jnp.maximum(m_old, jnp.max(bmax, axis=-1, keepdims=True))
    alpha = jnp.exp2(m_old - m_new)
    p_blocks = [jnp.exp2(sb - m_new) for sb in _lane_blocks(s)]
    l_ref[...] = alpha * l_ref[...] + functools.reduce(jnp.add, p_blocks)
    pv = pv_fn(jnp.concatenate(p_blocks, axis=1).astype(BF16))
    acc_ref[...] = jnp.concatenate([alpha] * (pv.shape[1] // LANES), axis=1) * acc_ref[...] + pv
    m_ref[...] = m_new


def _bias_by_distance(rel_bias, n_max):
    n = jnp.arange(n_max)
    max_exact = N_BUCKETS // 2
    large = max_exact + (jnp.log(jnp.maximum(n, 1).astype(F32) / max_exact)
                         / math.log(MAX_DISTANCE / max_exact) * (N_BUCKETS - max_exact)).astype(jnp.int32)
    large = jnp.minimum(large, N_BUCKETS - 1)
    bucket = jnp.where(n < max_exact, n, large)
    return jnp.transpose(rel_bias[bucket].astype(F32), (1, 0))


def _toeplitz_blocks(bdl):
    assert MAX_DISTANCE <= LANES
    per = 2 * LANES
    n = np.arange(per)
    blocks = []
    for e in range(2):
        idx = np.where(n < LANES, np.maximum(LANES * e - n, 0), LANES * e + per - n)
        w = bdl[:, idx]
        x = jnp.tile(w, (1, LANES))[:, :LANES * (per - 1)].reshape(-1, LANES, per - 1)
        blocks.append(x[:, :, :LANES])
    return jnp.stack(blocks, axis=1)


def _attn_prompt_kernel(cfar_ref, q_ref, k_ref, v_ref, u_ref, szb_ref, sub_ref, lq1_ref, lk1_ref, lq2_ref, lk2_ref,
                        out_ref, tz_ref, s_ref, bm_ref, m_ref, l_ref, acc_ref, *, tq, tk):
    h = pl.program_id(1)
    qi = pl.program_id(2)
    nblk = tq // LANES

    @pl.when(qi == 0)
    def _():
        const = jnp.full((tq, tk), cfar_ref[h], F32)
        for d in range(3):
            tz_ref[d] = const
        tz_ref[1, 0:LANES, tk - LANES:tk] = u_ref[0, 1]
        for bi in range(nblk):
            tz_ref[0, bi * LANES:(bi + 1) * LANES, bi * LANES:(bi + 1) * LANES] = u_ref[0, 0]
            if bi >= 1:
                tz_ref[0, bi * LANES:(bi + 1) * LANES, (bi - 1) * LANES:bi * LANES] = u_ref[0, 1]

    q = q_ref[0]
    qs = (q[:, :HEAD_DIM], q[:, HEAD_DIM:])
    m_ref[...] = jnp.full(m_ref.shape, NEG_INF, F32)
    l_ref[...] = jnp.zeros(l_ref.shape, F32)
    acc_ref[...] = jnp.zeros(acc_ref.shape, F32)

    def scores(j, slot):
        off = pl.multiple_of(j * tk, tk)
        kj = k_ref[0, pl.ds(off, tk), :]
        bias = tz_ref[jnp.minimum(qi - j, 2)]
        for c in range(2):
            s = lax.dot_general(qs[c], kj[:, c * HEAD_DIM:(c + 1) * HEAD_DIM], (((1,), (1,)), ((), ())),
                                preferred_element_type=F32) + bias
            s_ref[slot, c] = s
            bm_ref[slot, c] = _block_max(s)

    def consume(j, slot, mask):
        off = pl.multiple_of(j * tk, tk)
        vj = v_ref[0, pl.ds(off, tk), :]
        for c in range(2):
            s = s_ref[slot, c]
            if mask is None:
                bmax = bm_ref[slot, c]
            else:
                s = jnp.where(mask, s, NEG_INF)
                bmax = _block_max(s)
            _softmax_update(s, bmax, m_ref.at[c], l_ref.at[c], acc_ref.at[c],
                            lambda p: jnp.dot(p, vj, preferred_element_type=F32))

    scores(0, 0)

    def body(j, carry):
        consume(j, j % 2, None)
        scores(j + 1, (j + 1) % 2)
        return carry

    lax.fori_loop(0, qi, body, 0)

    row = lax.broadcasted_iota(jnp.int32, (tq, tk), 0)
    col = lax.broadcasted_iota(jnp.int32, (tq, tk), 1)
    consume(qi, qi % 2, col <= row)

    lam = _lambda_value(lq1_ref, lk1_ref, lq2_ref, lk2_ref)
    o1 = acc_ref[0] / jnp.sum(l_ref[0], axis=-1, keepdims=True)
    o2 = acc_ref[1] / jnp.sum(l_ref[1], axis=-1, keepdims=True)
    out_ref[0] = _subln_gate(o1 - lam * o2, sub_ref[...], szb_ref[0])


def _attn_prompt(q_bf, k_bf, v_bf, szb, bdl, sub, lq1, lk1, lq2, lk2):
    b, t, _ = q_bf.shape
    tq, tk = ATT_TQ, ATT_TK
    assert tq == tk and tq % LANES == 0 and t % tq == 0
    u = _toeplitz_blocks(bdl)
    cfar = bdl[:, MAX_DISTANCE]
    smem = pl.BlockSpec(memory_space=pltpu.SMEM)
    vec = pl.BlockSpec((1, HEAD_DIM), lambda bi, h, i: (0, 0))
    vmem = (2 * (2 * t * V_DIM * 2 + 2 * LANES * LANES * 4 + tq * V_DIM * (2 + 4 + 2))
            + (3 + 2 * 2) * tq * tk * 4 + (2 * 2 + 2 * 2) * tq * LANES * 4 + 2 * tq * V_DIM * 4 + 6 * tq * tk * 4)
    return pl.pallas_call(
        functools.partial(_attn_prompt_kernel, tq=tq, tk=tk),
        grid=(b, N_HEADS, t // tq),
        in_specs=[
            smem,
            pl.BlockSpec((1, tq, V_DIM), lambda bi, h, i: (bi, i, h)),
            pl.BlockSpec((1, t, V_DIM), lambda bi, h, i: (bi, 0, h)),
            pl.BlockSpec((1, t, V_DIM), lambda bi, h, i: (bi, 0, h)),
            pl.BlockSpec((1, 2, LANES, LANES), lambda bi, h, i: (h, 0, 0, 0)),
            pl.BlockSpec((1, tq, V_DIM), lambda bi, h, i: (bi, i, h)),
            pl.BlockSpec((1, V_DIM), lambda bi, h, i: (0, 0)),
            vec, vec, vec, vec,
        ],
        out_specs=pl.BlockSpec((1, tq, V_DIM), lambda bi, h, i: (bi, i, h)),
        out_shape=jax.ShapeDtypeStruct((b, t, N_HEADS * V_DIM), BF16),
        scratch_shapes=[pltpu.VMEM((3, tq, tk), F32), pltpu.VMEM((2, 2, tq, tk), F32),
                        pltpu.VMEM((2, 2, tq, LANES), F32), pltpu.VMEM((2, tq, LANES), F32),
                        pltpu.VMEM((2, tq, LANES), F32), pltpu.VMEM((2, tq, V_DIM), F32)],
        compiler_params=_cparams(("arbitrary", "arbitrary", "arbitrary"), vmem // MIB + 6),
        name="attn_prompt",
    )(cfar, q_bf, k_bf, v_bf, u, szb, sub, lq1, lk1, lq2, lk2)


def _attn_decode_kernel(pt_ref, qc_ref, bias_ref, knew_ref, vnew_ref, bnew_ref, szb_ref, sub_ref,
                        lq1_ref, lk1_ref, lq2_ref, lk2_ref, *refs, n_pages, t_new):
    del pt_ref
    k_refs, v_refs = refs[:n_pages], refs[n_pages:2 * n_pages]
    out_ref, m_ref, l_ref, acc_ref = refs[2 * n_pages:]
    s_idx = pl.program_id(1)
    nrow = qc_ref.shape[2]
    cols = k_refs[0].shape[1] // 2
    chunk = n_pages * cols
    nt = (((1,), (1,)), ((), ()))
    qcs = (qc_ref[0, 0], qc_ref[0, 1])

    @pl.when(s_idx == 0)
    def _():
        m_ref[...] = jnp.full(m_ref.shape, NEG_INF, F32)
        l_ref[...] = jnp.zeros(l_ref.shape, F32)
        acc_ref[...] = jnp.zeros(acc_ref.shape, F32)

    bias = bias_ref[:, pl.ds(pl.multiple_of(s_idx * chunk, chunk), chunk)]
    s_maps = []
    for c in range(2):
        parts = [lax.dot_general(qcs[c], k_refs[i][0, pl.ds(c, cols, stride=2), :].astype(BF16), nt,
                                 preferred_element_type=F32) for i in range(n_pages)]
        s_maps.append(jnp.concatenate(parts, axis=1) + bias)
    s_past = jnp.concatenate(s_maps, axis=0)

    def pv_past(p):
        out = jnp.dot(p[:, 0:cols], v_refs[0][0].astype(BF16), preferred_element_type=F32)
        for i in range(1, n_pages):
            out = out + jnp.dot(p[:, i * cols:(i + 1) * cols], v_refs[i][0].astype(BF16), preferred_element_type=F32)
        return out

    _softmax_update(s_past, _block_max(s_past), m_ref, l_ref, acc_ref, pv_past)

    @pl.when(s_idx == pl.num_programs(1) - 1)
    def _():
        s_new = jnp.concatenate([lax.dot_general(qcs[c], knew_ref[0, c], nt, preferred_element_type=F32) + bnew_ref[...]
                                 for c in range(2)], axis=0)
        row = lax.broadcasted_iota(jnp.int32, s_new.shape, 0)
        col = lax.broadcasted_iota(jnp.int32, s_new.shape, 1)
        valid = (col % N_HEADS == (row % nrow) // t_new) & (col // N_HEADS <= row % t_new)
        s_new = jnp.where(valid, s_new, NEG_INF)
        _softmax_update(s_new, _block_max(s_new), m_ref, l_ref, acc_ref,
                        lambda p: jnp.dot(p, vnew_ref[0], preferred_element_type=F32))

        lam = _lambda_value(lq1_ref, lk1_ref, lq2_ref, lk2_ref)
        o = acc_ref[...] / jnp.sum(l_ref[...], axis=-1, keepdims=True)
        out_ref[0] = _subln_gate(o[0:nrow] - lam * o[nrow:2 * nrow], sub_ref[...], szb_ref[0])


def _attn_decode(page_table, cache_kr, cache_vr, qc, bias_past, k_new, v_new, bias_new, szb, sub, lq1, lk1, lq2, lk2):
    nb, n_pt = page_table.shape
    _, k_rows, _ = cache_kr.shape
    _, v_rows, _ = cache_vr.shape
    nrow = qc.shape[2]
    t_new = nrow // N_HEADS
    npg = DEC_PAGES
    assert n_pt % npg == 0 and k_rows == 2 * v_rows
    new_rows = k_new.shape[2]

    def page_spec(rows, width, i):
        return pl.BlockSpec((1, rows, width), lambda b, s, pt, i=i: (pt[b, s * npg + i], 0, 0))

    vec = pl.BlockSpec((1, HEAD_DIM), lambda b, s, pt: (0, 0))
    in_specs = [
        pl.BlockSpec((1, 2, nrow, HEAD_DIM), lambda b, s, pt: (b, 0, 0, 0)),
        pl.BlockSpec(bias_past.shape, lambda b, s, pt: (0, 0), pipeline_mode=pl.Buffered(1)),
        pl.BlockSpec((1, 2, new_rows, HEAD_DIM), lambda b, s, pt: (b, 0, 0, 0)),
        pl.BlockSpec((1, new_rows, V_DIM), lambda b, s, pt: (b, 0, 0)),
        pl.BlockSpec((nrow, new_rows), lambda b, s, pt: (0, 0)),
        pl.BlockSpec((1, nrow, V_DIM), lambda b, s, pt: (b, 0, 0)),
        pl.BlockSpec((1, V_DIM), lambda b, s, pt: (0, 0)),
        vec, vec, vec, vec,
    ] + [page_spec(k_rows, HEAD_DIM, i) for i in range(npg)] + [page_spec(v_rows, V_DIM, i) for i in range(npg)]
    page_bytes = k_rows * HEAD_DIM * 4
    vmem = 2 * 2 * npg * page_bytes + bias_past.size * 4 + 10 * 2 * nrow * npg * v_rows * 4
    return pl.pallas_call(
        functools.partial(_attn_decode_kernel, n_pages=npg, t_new=t_new),
        grid_spec=pltpu.PrefetchScalarGridSpec(
            num_scalar_prefetch=1,
            grid=(nb, n_pt // npg),
            in_specs=in_specs,
            out_specs=pl.BlockSpec((1, nrow, V_DIM), lambda b, s, pt: (b, 0, 0)),
            scratch_shapes=[pltpu.VMEM((2 * nrow, LANES), F32), pltpu.VMEM((2 * nrow, LANES), F32),
                            pltpu.VMEM((2 * nrow, V_DIM), F32)],
        ),
        out_shape=jax.ShapeDtypeStruct((nb, nrow, V_DIM), BF16),
        compiler_params=_cparams(("arbitrary", "arbitrary"), vmem // MIB + 6),
        name="attn_decode",
    )(page_table, qc, bias_past, k_new, v_new, bias_new, szb, sub, lq1, lk1, lq2, lk2,
      *([cache_kr] * npg), *([cache_vr] * npg))


def _final_kernel(hb_ref, ma_ref, sgb_ref, x_ref, p_ref, wpb_ref, wout_ref, wgate_ref, wple_ref, lng_ref, lnb_ref, y_ref):
    b_proj = jnp.dot(hb_ref[...], wpb_ref[...], preferred_element_type=F32)
    merged = ma_ref[...] + sgb_ref[...] * b_proj
    z = ALPHA * x_ref[...] + jnp.dot(merged.astype(BF16), wout_ref[...], preferred_element_type=F32)
    mu = jnp.mean(z, axis=-1, keepdims=True)
    zc = z - mu
    var = jnp.mean(zc * zc, axis=-1, keepdims=True)
    h = zc * lax.rsqrt(var + LN_EPS) * lng_ref[...] + lnb_ref[...]
    gate = jax.nn.sigmoid(jnp.dot(h.astype(BF16), wgate_ref[...], preferred_element_type=F32))
    pe = jnp.dot(p_ref[...].astype(BF16), wple_ref[...], preferred_element_type=F32)
    y_ref[...] = h + gate * pe


def _final(hb, ma, sgab, x, p, wpb_bf, wout_bf, wgate_bf, wple_bf, lng, lnb, tm):
    m, d = x.shape
    pd = p.shape[1]
    nblk_d = 1
    row = lambda w: pl.BlockSpec((tm, w), lambda i: (i, 0))
    const = lambda r, c: pl.BlockSpec((r, c), lambda i: (0, 0), pipeline_mode=pl.Buffered(1))
    vmem = 2 * tm * (d * 2 + 4 * d * 4 + pd * 4) + (3 * d * d + pd * d) * 2 + 6 * tm * d * 4
    return pl.pallas_call(
        _final_kernel,
        grid=(m // tm,),
        in_specs=[row(d), row(d), pl.BlockSpec((tm, d), lambda i: (i, nblk_d)), row(d), row(pd),
                  const(d, d), const(d, d), const(d, d), const(pd, d), const(1, d), const(1, d)],
        out_specs=row(d),
        out_shape=jax.ShapeDtypeStruct((m, d), F32),
        compiler_params=_cparams(("arbitrary",), vmem // MIB + 6),
        name=f"final_m{m}",
    )(hb, ma, sgab, x, p, wpb_bf, wout_bf, wgate_bf, wple_bf, lng, lnb)


def kernel(x_prompt, x_sample, p_prompt, p_sample, cache_k, cache_v, state_conv, page_table, w_in, conv_w, conv_b, conv_ln_g, conv_ln_b, w_proj_a, lambda_q1, lambda_k1, lambda_q2, lambda_k2, subln_w, w_proj_b, w_out, ln_g, ln_b, w_ple_proj, w_ple_gate, rel_bias):
    assert w_in.shape[0] == DEPTH
    b, t, d = x_prompt.shape
    nb, tn, _ = x_sample.shape
    c = conv_w.shape[-1]
    n_maps = 2 * N_HEADS
    qk_w = n_maps * HEAD_DIM
    att_w = N_HEADS * V_DIM
    pd = p_prompt.shape[-1]
    n_phys, page = cache_k.shape[1], cache_k.shape[2]
    past = page_table.shape[1] * page

    w_bf = w_in[0].astype(BF16)
    wpa_bf = w_proj_a[0].astype(BF16)
    wpb_bf = w_proj_b[0].astype(BF16)
    wout_bf = w_out[0].astype(BF16)
    wgate_bf = w_ple_gate[0].astype(BF16)
    wple_bf = w_ple_proj[0].astype(BF16)
    cw, cb = conv_w[0], conv_b[0].reshape(1, c)
    clng, clnb = conv_ln_g[0].reshape(1, c), conv_ln_b[0].reshape(1, c)
    lng, lnb = ln_g[0].reshape(1, d), ln_b[0].reshape(1, d)
    sub = subln_w[0].reshape(1, V_DIM)
    lq1, lk1, lq2, lk2 = (v[0].reshape(1, HEAD_DIM) for v in (lambda_q1, lambda_k1, lambda_q2, lambda_k2))
    bdl = _bias_by_distance(rel_bias, past + tn) * LOG2E

    xp = x_prompt.reshape(b * t, d)
    g, sza, q_bf, k_p, k_bf, v_p, v_bf, szb, sgab = _in_projection(xp.astype(BF16), w_bf, PROJ_TM, c, qk_w, att_w, d)
    g3 = g.reshape(b, t, c)
    ma = _conv_prompt(g3, jnp.zeros((b, CONV_HALO, c), F32), cw, cb, clng, clnb, sza.reshape(b, t, c),
                      sgab.reshape(b, t, 2 * d), wpa_bf)
    hb = _attn_prompt(q_bf.reshape(b, t, qk_w), k_bf.reshape(b, t, qk_w), v_bf.reshape(b, t, att_w),
                      szb.reshape(b, t, att_w), bdl, sub, lq1, lk1, lq2, lk2)
    y_p = _final(hb.reshape(b * t, att_w), ma.reshape(b * t, d), sgab, xp, p_prompt[0].reshape(b * t, pd),
                 wpb_bf, wout_bf, wgate_bf, wple_bf, lng, lnb, FINAL_TM)

    ms = nb * tn
    xs = x_sample.reshape(ms, d)
    g_s, sza_s, q_s, k_s, k_s_bf, v_s, v_s_bf, szb_s, sgab_s = _in_projection(xs.astype(BF16), w_bf, ms, c, qk_w, att_w, d)
    tm_major = lambda a: jnp.transpose(a.reshape(nb, tn, a.shape[-1]), (1, 0, 2))
    xc_tm = jnp.concatenate([jnp.transpose(state_conv[0], (1, 0, 2)), tm_major(g_s)], axis=0)
    ma_s_tm = _conv_sample(xc_tm, cw, cb, clng, clnb, tm_major(sza_s).reshape(ms, c),
                           tm_major(sgab_s[:, :d]).reshape(ms, d), wpa_bf, tn)
    ma_s = jnp.transpose(ma_s_tm.reshape(tn, nb, d), (1, 0, 2)).reshape(ms, d)

    nrow = N_HEADS * tn
    qc = jnp.transpose(q_s.reshape(nb, tn, N_HEADS, 2, HEAD_DIM), (0, 3, 2, 1, 4)).reshape(nb, 2, nrow, HEAD_DIM)
    k_new = jnp.transpose(k_s_bf.reshape(nb, tn, N_HEADS, 2, HEAD_DIM), (0, 3, 1, 2, 4)).reshape(nb, 2, nrow, HEAD_DIM)
    k_new = jnp.pad(k_new, ((0, 0), (0, 0), (0, LANES - nrow), (0, 0)))
    v_new = jnp.pad(v_s_bf.reshape(nb, nrow, V_DIM), ((0, 0), (0, LANES - nrow), (0, 0)))
    szb_hq = jnp.transpose(szb_s.reshape(nb, tn, N_HEADS, V_DIM), (0, 2, 1, 3)).reshape(nb, nrow, V_DIM)
    same_head = jnp.eye(N_HEADS, dtype=bool)[:, None, None, :]
    bd_past = jnp.stack([jnp.flip(bdl[:, qi + 1:qi + 1 + past], axis=1) for qi in range(tn)], axis=1)
    bias_past = jnp.where(same_head, bd_past[..., None], NEG_INF).reshape(nrow, past * N_HEADS)
    new_tok = LANES // N_HEADS
    dist_new = np.maximum(np.arange(tn)[:, None] - np.arange(new_tok)[None, :], 0)
    bias_new = jnp.where(same_head, bdl[:, dist_new][..., None], 0.0).reshape(nrow, LANES)
    hq = _attn_decode(page_table, cache_k.reshape(n_phys, page * n_maps, HEAD_DIM),
                      cache_v.reshape(n_phys, page * N_HEADS, V_DIM), qc, bias_past, k_new, v_new, bias_new,
                      szb_hq, sub, lq1, lk1, lq2, lk2)
    hb_s = jnp.transpose(hq.reshape(nb, N_HEADS, tn, V_DIM), (0, 2, 1, 3)).reshape(ms, att_w)
    y_s = _final(hb_s, ma_s, sgab_s, xs, p_sample[0].reshape(ms, pd),
                 wpb_bf, wout_bf, wgate_bf, wple_bf, lng, lnb, ms)

    conv_prompt = g3[:, t - (CONV_WIDTH - 1):, :]
    conv_sample = jnp.concatenate([state_conv[0][:, tn:, :], g_s.reshape(nb, tn, c)], axis=1)
    return (y_p.reshape(b, t, d), y_s.reshape(nb, tn, d),
            k_p.reshape(1, b, t, n_maps, HEAD_DIM), v_p.reshape(1, b, t, N_HEADS, V_DIM), conv_prompt[None],
            k_s.reshape(1, nb, tn, n_maps, HEAD_DIM), v_s.reshape(1, nb, tn, N_HEADS, V_DIM), conv_sample[None])
```

```python
import functools
import math

import numpy as np
import jax
import jax.numpy as jnp
from jax import lax
from jax.experimental import pallas as pl
from jax.experimental.pallas import tpu as pltpu

F32 = jnp.float32
BF16 = jnp.bfloat16

N_HEADS = 8
HEAD_DIM = 128
V_DIM = 2 * HEAD_DIM
CONV_WIDTH = 31
N_BUCKETS = 32
MAX_DISTANCE = 128
LN_EPS = 1e-5
NEG_INF = -1e30
DEPTH = 1
ALPHA = (2 * DEPTH) ** 0.25
LAM_INIT = 0.8 - 0.6 * math.exp(-0.3 * 0)
QK_SCALE = HEAD_DIM ** -0.5
LOG2E = math.log2(math.e)

V7X_VMEM_BYTES = 64 * 1024 * 1024
V7X_SUBLANES = 8
LANES = 128
MIB = 1024 * 1024

PROJ_TN = 1024
PROJ_TM = 1024
GLU_TM = 512
CONV_TC = 256
CONV_RB = 32
CONV_HALO = 32
ATT_TQ = 512
ATT_TK = 512
DEC_PAGES = 8
FINAL_TM = 256


def _cparams(sem, vmem_mib):
    limit = min(vmem_mib * MIB, V7X_VMEM_BYTES - 6 * MIB)
    return pltpu.CompilerParams(dimension_semantics=sem, vmem_limit_bytes=limit)


def _proj_kernel(x_ref, *refs, n_w, epilogue):
    w_refs, out_refs = refs[:n_w], refs[n_w:]
    x = x_ref[...]
    accs = [jnp.dot(x, w[...], preferred_element_type=F32) for w in w_refs]
    for o_ref, o in zip(out_refs, epilogue(*accs)):
        o_ref[...] = o.astype(o_ref.dtype)


def _proj(name, x_bf, w_bf, col_starts, ncols, epilogue, out_dtypes, tm):
    m, k = x_bf.shape
    tn = PROJ_TN
    n_w = len(col_starts)
    in_specs = [pl.BlockSpec((tm, k), lambda j, i: (i, 0))]
    for c0 in col_starts:
        assert c0 % tn == 0
        in_specs.append(pl.BlockSpec((k, tn), lambda j, i, c0=c0: (0, c0 // tn + j)))
    out_specs = [pl.BlockSpec((tm, tn), lambda j, i: (i, j)) for _ in out_dtypes]
    out_shape = [jax.ShapeDtypeStruct((m, ncols), dt) for dt in out_dtypes]
    vmem = 2 * (tm * k * 2 + n_w * k * tn * 2 + sum(tm * tn * jnp.dtype(d).itemsize for d in out_dtypes))
    vmem += n_w * tm * tn * 4
    return pl.pallas_call(
        functools.partial(_proj_kernel, n_w=n_w, epilogue=epilogue),
        grid=(ncols // tn, m // tm),
        in_specs=in_specs, out_specs=out_specs, out_shape=out_shape,
        compiler_params=_cparams(("arbitrary", "arbitrary"), vmem // MIB + 6),
        name=f"proj_{name}_m{m}",
    )(x_bf, *([w_bf] * n_w))


def _ep_silu(z):
    return (jax.nn.silu(z),)


def _ep_sigmoid(z):
    return (jax.nn.sigmoid(z),)


def _ep_query(z):
    return (z * (QK_SCALE * LOG2E),)


def _ep_copy2(z):
    return (z, z)


def _glu_cast_kernel(x_ref, wa_ref, wg_ref, g_ref, xbf_ref):
    xb = x_ref[...].astype(BF16)
    xbf_ref[...] = xb
    ua = jnp.dot(xb, wa_ref[...], preferred_element_type=F32)
    ug = jnp.dot(xb, wg_ref[...], preferred_element_type=F32)
    g_ref[...] = ua * jax.nn.sigmoid(ug)


def _proj_glu_cast(x_f32, w_bf, c_conv):
    m, k = x_f32.shape
    tm, tn = GLU_TM, PROJ_TN
    assert c_conv == tn and m % tm == 0
    vmem = 2 * (tm * k * 4 + 2 * k * tn * 2 + tm * tn * 4 + tm * k * 2) + 2 * tm * tn * 4 + tm * k * 2
    return pl.pallas_call(
        _glu_cast_kernel,
        grid=(m // tm,),
        in_specs=[pl.BlockSpec((tm, k), lambda i: (i, 0)), pl.BlockSpec((k, tn), lambda i: (0, 0)),
                  pl.BlockSpec((k, tn), lambda i: (0, 1))],
        out_specs=[pl.BlockSpec((tm, tn), lambda i: (i, 0)), pl.BlockSpec((tm, k), lambda i: (i, 0))],
        out_shape=[jax.ShapeDtypeStruct((m, c_conv), F32), jax.ShapeDtypeStruct((m, k), BF16)],
        compiler_params=_cparams(("arbitrary",), vmem // MIB + 6),
        name="proj_glu_cast",
    )(x_f32, w_bf, w_bf)


def _in_projection(x_f32, w_bf, tm, c_conv, qk_w, att_w, d_model):
    g, x_bf = _proj_glu_cast(x_f32, w_bf, c_conv)
    o = 2 * c_conv
    (sza,) = _proj("za", x_bf, w_bf, (o,), c_conv, _ep_silu, (F32,), tm)
    o += c_conv
    (q_bf,) = _proj("q", x_bf, w_bf, (o,), qk_w, _ep_query, (BF16,), tm)
    o += qk_w
    k, k_bf = _proj("k", x_bf, w_bf, (o,), qk_w, _ep_copy2, (F32, BF16), tm)
    o += qk_w
    v, v_bf = _proj("v", x_bf, w_bf, (o,), att_w, _ep_copy2, (F32, BF16), tm)
    o += att_w
    (szb,) = _proj("zb", x_bf, w_bf, (o,), att_w, _ep_silu, (F32,), tm)
    o += att_w
    (sgab,) = _proj("gab", x_bf, w_bf, (o,), 2 * d_model, _ep_sigmoid, (F32,), tm)
    return g, sza, q_bf, k, k_bf, v, v_bf, szb, sgab


def _proj_small_kernel(x_ref, w_ref, o_ref, wbf_ref, *, silu_tiles, sigmoid_tiles, query_tiles):
    j = pl.program_id(0)
    in_range = lambda r: (j >= r[0]) & (j < r[1])
    any_range = lambda rs: functools.reduce(jnp.logical_or, [in_range(r) for r in rs])
    w = w_ref[...].astype(BF16)
    wbf_ref[...] = w
    z = jnp.dot(x_ref[...], w, preferred_element_type=F32)
    sg = jax.nn.sigmoid(z)
    o_ref[...] = jnp.where(any_range(silu_tiles), z * sg,
                           jnp.where(any_range(sigmoid_tiles), sg,
                                     jnp.where(any_range(query_tiles), z * (QK_SCALE * LOG2E), z)))


def _in_projection_small(x_bf, w_f32, c_conv, qk_w, att_w, d_model):
    m, k = x_bf.shape
    tn = PROJ_TN
    sizes = (c_conv, c_conv, c_conv, qk_w, qk_w, att_w, att_w, d_model, d_model)
    starts = np.concatenate([[0], np.cumsum(sizes)])
    assert all(s % tn == 0 for s in sizes)
    tiles = lambda i: (int(starts[i]) // tn, int(starts[i + 1]) // tn)
    n_in = int(starts[-1])
    xin, w_bf = pl.pallas_call(
        functools.partial(_proj_small_kernel, silu_tiles=(tiles(2), tiles(6)), sigmoid_tiles=(tiles(7), tiles(8)),
                          query_tiles=(tiles(3),)),
        grid=(n_in // tn,),
        in_specs=[pl.BlockSpec((m, k), lambda j: (0, 0)), pl.BlockSpec((k, tn), lambda j: (0, j))],
        out_specs=[pl.BlockSpec((m, tn), lambda j: (0, j)), pl.BlockSpec((k, tn), lambda j: (0, j))],
        out_shape=[jax.ShapeDtypeStruct((m, n_in), F32), jax.ShapeDtypeStruct((k, n_in), BF16)],
        compiler_params=_cparams(("arbitrary",), (2 * (m * k * 2 + k * tn * 6 + m * tn * 4) + k * tn * 2 + 4 * m * tn * 4) // MIB + 6),
        name="proj_small",
    )(x_bf, w_f32)
    return w_bf, tuple(xin[:, int(starts[i]):int(starts[i + 1])] for i in range(len(sizes)))


def _ln_swish(acc, lng, lnb):
    mu = jnp.mean(acc, axis=-1, keepdims=True)
    xc = acc - mu
    var = jnp.mean(xc * xc, axis=-1, keepdims=True)
    y = xc * lax.rsqrt(var + LN_EPS) * lng + lnb
    return y * jax.nn.sigmoid(y)


def _conv_prompt_kernel(g_ref, halo_ref, prev_ref, cw_ref, cb_ref, lng_ref, lnb_ref, sza_ref, sga_ref, wpa_ref,
                        out_ref, xs_ref, hs_ref, *, tc, rb):
    i = pl.program_id(1)
    n = tc + CONV_HALO
    xs_ref[0, 0:CONV_HALO, :] = jnp.where(i == 0, prev_ref[0], halo_ref[0])
    xs_ref[0, CONV_HALO:n, :] = g_ref[0]
    for r in range(1, V7X_SUBLANES):
        xs_ref[r, 0:n - V7X_SUBLANES, :] = xs_ref[0, r:r + n - V7X_SUBLANES, :]

    first = CONV_HALO - (CONV_WIDTH - 1)

    def body(tb, carry):
        t0 = pl.multiple_of(tb * rb, rb)
        acc = jnp.broadcast_to(cb_ref[...], (rb, cb_ref.shape[-1]))
        for j in range(CONV_WIDTH):
            r, a = (first + j) % V7X_SUBLANES, (first + j) // V7X_SUBLANES
            acc = acc + cw_ref[j:j + 1, :] * xs_ref[r, pl.ds(t0 + V7X_SUBLANES * a, rb), :]
        h = _ln_swish(acc, lng_ref[...], lnb_ref[...]) * sza_ref[0, pl.ds(t0, rb), :]
        hs_ref[pl.ds(t0, rb), :] = h.astype(BF16)
        return carry

    lax.fori_loop(0, tc // rb, body, 0)
    a_proj = jnp.dot(hs_ref[...], wpa_ref[...], preferred_element_type=F32)
    out_ref[0] = sga_ref[0] * a_proj


def _conv_prompt(g, prev, conv_w, conv_b, lng, lnb, sza, sgab, wpa_bf):
    b, t, c = g.shape
    d = wpa_bf.shape[1]
    tc, rb = CONV_TC, CONV_RB
    hpt = tc // CONV_HALO
    vmem = 2 * (2 * tc * c * 4 + tc * d * 4 * 2 + c * d * 2) + 8 * (tc + CONV_HALO) * c * 4 + tc * c * 2 + tc * d * 4
    return pl.pallas_call(
        functools.partial(_conv_prompt_kernel, tc=tc, rb=rb),
        grid=(b, t // tc),
        in_specs=[
            pl.BlockSpec((1, tc, c), lambda bi, i: (bi, i, 0)),
            pl.BlockSpec((1, CONV_HALO, c), lambda bi, i: (bi, jnp.maximum(i * hpt - 1, 0), 0)),
            pl.BlockSpec((1, CONV_HALO, c), lambda bi, i: (bi, 0, 0)),
            pl.BlockSpec((CONV_WIDTH, c), lambda bi, i: (0, 0)),
            pl.BlockSpec((1, c), lambda bi, i: (0, 0)),
            pl.BlockSpec((1, c), lambda bi, i: (0, 0)),
            pl.BlockSpec((1, c), lambda bi, i: (0, 0)),
            pl.BlockSpec((1, tc, c), lambda bi, i: (bi, i, 0)),
            pl.BlockSpec((1, tc, d), lambda bi, i: (bi, i, 0)),
            pl.BlockSpec((c, d), lambda bi, i: (0, 0)),
        ],
        out_specs=pl.BlockSpec((1, tc, d), lambda bi, i: (bi, i, 0)),
        out_shape=jax.ShapeDtypeStruct((b, t, d), F32),
        scratch_shapes=[pltpu.VMEM((V7X_SUBLANES, tc + CONV_HALO, c), F32), pltpu.VMEM((tc, c), BF16)],
        compiler_params=_cparams(("arbitrary", "arbitrary"), vmem // MIB + 6),
        name="conv_prompt",
    )(g, g, prev, conv_w, conv_b, lng, lnb, sza, sgab, wpa_bf)


def _conv_sample_kernel(state_ref, ua_ref, ug_ref, cw_ref, cb_ref, lng_ref, lnb_ref, sza_ref, sga_ref, wpa_ref,
                        out_ref, g_ref, hs_ref):
    n_state, nb, _ = state_ref.shape
    t_new = ua_ref.shape[0]
    g_ref[...] = ua_ref[...] * jax.nn.sigmoid(ug_ref[...])
    window = lambda r: state_ref[r] if r < n_state else g_ref[r - n_state]
    for t in range(t_new):
        acc = jnp.broadcast_to(cb_ref[...], (nb, cb_ref.shape[-1]))
        for j in range(CONV_WIDTH):
            acc = acc + cw_ref[j:j + 1, :] * window(t + j)
        h = _ln_swish(acc, lng_ref[...], lnb_ref[...]) * sza_ref[t * nb:(t + 1) * nb, :]
        hs_ref[t * nb:(t + 1) * nb, :] = h.astype(BF16)
    out_ref[...] = sga_ref[...] * jnp.dot(hs_ref[...], wpa_ref[...], preferred_element_type=F32)


def _conv_sample(state_tm, ua_tm, ug_tm, conv_w, conv_b, lng, lnb, sza_tm, sga_tm, wpa_bf):
    t_new, nb, c = ua_tm.shape
    d = wpa_bf.shape[1]
    assert state_tm.shape[0] == CONV_WIDTH - 1
    return pl.pallas_call(
        _conv_sample_kernel,
        out_shape=[jax.ShapeDtypeStruct((t_new * nb, d), F32), jax.ShapeDtypeStruct((t_new, nb, c), F32)],
        scratch_shapes=[pltpu.VMEM((t_new * nb, c), BF16)],
        compiler_params=pltpu.CompilerParams(vmem_limit_bytes=32 * MIB),
        name="conv_sample",
    )(state_tm, ua_tm, ug_tm, conv_w, conv_b, lng, lnb, sza_tm, sga_tm, wpa_bf)


def _lambda_value(lq1_ref, lk1_ref, lq2_ref, lk2_ref):
    s1 = jnp.sum(lq1_ref[...] * lk1_ref[...], axis=-1, keepdims=True)
    s2 = jnp.sum(lq2_ref[...] * lk2_ref[...], axis=-1, keepdims=True)
    return jnp.exp(s1) - jnp.exp(s2) + LAM_INIT


def _subln_gate(o, sub, szb):
    r = o * lax.rsqrt(jnp.mean(o * o, axis=-1, keepdims=True) + LN_EPS) * sub * (1.0 - LAM_INIT)
    return (r * szb).astype(BF16)


def _lane_blocks(x):
    return [x[:, i * LANES:(i + 1) * LANES] for i in range(x.shape[1] // LANES)]


def _block_max(s):
    return functools.reduce(jnp.maximum, _lane_blocks(s))


def _softmax_update(s, bmax, m_ref, l_ref, acc_ref, pv_fn):
    m_old = m_ref[...]
    m_new = jnp.maximum(m_old, jnp.max(bmax, axis=-1, keepdims=True))
    alpha = jnp.exp2(m_old - m_new)
    p_blocks = [jnp.exp2(sb - m_new) for sb in _lane_blocks(s)]
    l_ref[...] = alpha * l_ref[...] + functools.reduce(jnp.add, p_blocks)
    pv = pv_fn(jnp.concatenate(p_blocks, axis=1).astype(BF16))
    acc_ref[...] = jnp.concatenate([alpha] * (pv.shape[1] // LANES), axis=1) * acc_ref[...] + pv
    m_ref[...] = m_new


def _bias_by_distance(rel_bias, n_max):
    n = jnp.arange(n_max)
    max_exact = N_BUCKETS // 2
    large = max_exact + (jnp.log(jnp.maximum(n, 1).astype(F32) / max_exact)
                         / math.log(MAX_DISTANCE / max_exact) * (N_BUCKETS - max_exact)).astype(jnp.int32)
    large = jnp.minimum(large, N_BUCKETS - 1)
    bucket = jnp.where(n < max_exact, n, large)
    return jnp.transpose(rel_bias[bucket].astype(F32), (1, 0))


def _toeplitz_blocks(bdl):
    assert MAX_DISTANCE <= LANES
    per = 2 * LANES
    n = np.arange(per)
    blocks = []
    for e in range(2):
        idx = np.where(n < LANES, np.maximum(LANES * e - n, 0), LANES * e + per - n)
        w = bdl[:, idx]
        x = jnp.tile(w, (1, LANES))[:, :LANES * (per - 1)].reshape(-1, LANES, per - 1)
        blocks.append(x[:, :, :LANES])
    return jnp.stack(blocks, axis=1)


def _attn_prompt_kernel(cfar_ref, q_ref, k_ref, v_ref, u_ref, szb_ref, sub_ref, lq1_ref, lk1_ref, lq2_ref, lk2_ref,
                        out_ref, strip_ref, s_ref, bm_ref, m_ref, l_ref, acc_ref, *, tq, tk):
    h = pl.program_id(1)
    qi = pl.program_id(2)
    far = tq + 2 * LANES

    @pl.when(qi == 0)
    def _():
        strip_ref[...] = jnp.full(strip_ref.shape, cfar_ref[h], F32)
        strip_ref[tq:tq + LANES, :] = u_ref[0, 0]
        strip_ref[tq + LANES:far, :] = u_ref[0, 1]

    q = q_ref[0]
    qs = (q[:, :HEAD_DIM], q[:, HEAD_DIM:])
    m_ref[...] = jnp.full(m_ref.shape, NEG_INF, F32)
    l_ref[...] = jnp.zeros(l_ref.shape, F32)
    acc_ref[...] = jnp.zeros(acc_ref.shape, F32)

    def scores(j, slot):
        off = pl.multiple_of(j * tk, tk)
        kj = k_ref[0, pl.ds(off, tk), :]
        r0 = tq + (qi - j) * tk
        bias = [strip_ref[pl.ds(pl.multiple_of(jnp.minimum(r0 - cb * LANES, far), LANES), tq), :]
                for cb in range(tk // LANES)]
        for c in range(2):
            qk = lax.dot_general(qs[c], kj[:, c * HEAD_DIM:(c + 1) * HEAD_DIM], (((1,), (1,)), ((), ())),
                                 preferred_element_type=F32)
            s_blocks = [sb + bb for sb, bb in zip(_lane_blocks(qk), bias)]
            s_ref[slot, c] = jnp.concatenate(s_blocks, axis=1)
            bm_ref[slot, c] = functools.reduce(jnp.maximum, s_blocks)

    def consume(j, slot, mask):
        off = pl.multiple_of(j * tk, tk)
        vj = v_ref[0, pl.ds(off, tk), :]
        for c in range(2):
            s = s_ref[slot, c]
            if mask is None:
                bmax = bm_ref[slot, c]
            else:
                s = jnp.where(mask, s, NEG_INF)
                bmax = _block_max(s)
            _softmax_update(s, bmax, m_ref.at[c], l_ref.at[c], acc_ref.at[c],
                            lambda p: jnp.dot(p, vj, preferred_element_type=F32))

    scores(0, 0)

    def body(j, carry):
        consume(j, j % 2, None)
        scores(j + 1, (j + 1) % 2)
        return carry

    lax.fori_loop(0, qi, body, 0)

    row = lax.broadcasted_iota(jnp.int32, (tq, tk), 0)
    col = lax.broadcasted_iota(jnp.int32, (tq, tk), 1)
    consume(qi, qi % 2, col <= row)

    lam = _lambda_value(lq1_ref, lk1_ref, lq2_ref, lk2_ref)
    o1 = acc_ref[0] / jnp.sum(l_ref[0], axis=-1, keepdims=True)
    o2 = acc_ref[1] / jnp.sum(l_ref[1], axis=-1, keepdims=True)
    out_ref[0] = _subln_gate(o1 - lam * o2, sub_ref[...], szb_ref[0])


def _attn_prompt(q_bf, k_bf, v_bf, szb, bdl, sub, lq1, lk1, lq2, lk2):
    b, t, _ = q_bf.shape
    tq, tk = ATT_TQ, ATT_TK
    assert tq == tk and tq % LANES == 0 and t % tq == 0
    u = _toeplitz_blocks(bdl)
    cfar = bdl[:, MAX_DISTANCE]
    smem = pl.BlockSpec(memory_space=pltpu.SMEM)
    vec = pl.BlockSpec((1, HEAD_DIM), lambda bi, h, i: (0, 0))
    strip_rows = 2 * tq + 2 * LANES
    vmem = (2 * (2 * t * V_DIM * 2 + 2 * LANES * LANES * 4 + tq * V_DIM * (2 + 4 + 2))
            + 2 * 2 * tq * tk * 4 + (strip_rows + (2 * 2 + 2 * 2) * tq) * LANES * 4 + 2 * tq * V_DIM * 4
            + 4 * tq * tk * 4)
    return pl.pallas_call(
        functools.partial(_attn_prompt_kernel, tq=tq, tk=tk),
        grid=(b, N_HEADS, t // tq),
        in_specs=[
            smem,
            pl.BlockSpec((1, tq, V_DIM), lambda bi, h, i: (bi, i, h)),
            pl.BlockSpec((1, t, V_DIM), lambda bi, h, i: (bi, 0, h)),
            pl.BlockSpec((1, t, V_DIM), lambda bi, h, i: (bi, 0, h)),
            pl.BlockSpec((1, 2, LANES, LANES), lambda bi, h, i: (h, 0, 0, 0)),
            pl.BlockSpec((1, tq, V_DIM), lambda bi, h, i: (bi, i, h)),
            pl.BlockSpec((1, V_DIM), lambda bi, h, i: (0, 0)),
            vec, vec, vec, vec,
        ],
        out_specs=pl.BlockSpec((1, tq, V_DIM), lambda bi, h, i: (bi, i, h)),
        out_shape=jax.ShapeDtypeStruct((b, t, N_HEADS * V_DIM), BF16),
        scratch_shapes=[pltpu.VMEM((strip_rows, LANES), F32), pltpu.VMEM((2, 2, tq, tk), F32),
                        pltpu.VMEM((2, 2, tq, LANES), F32), pltpu.VMEM((2, tq, LANES), F32),
                        pltpu.VMEM((2, tq, LANES), F32), pltpu.VMEM((2, tq, V_DIM), F32)],
        compiler_params=_cparams(("arbitrary", "arbitrary", "arbitrary"), vmem // MIB + 6),
        name="attn_prompt",
    )(cfar, q_bf, k_bf, v_bf, u, szb, sub, lq1, lk1, lq2, lk2)


def _attn_decode_kernel(pt_ref, qc_ref, bfar_ref, bnear_ref, knew_ref, vnew_ref, bnew_ref, szb_ref, sub_ref,
                        lq1_ref, lk1_ref, lq2_ref, lk2_ref, *refs, n_pages, t_new):
    del pt_ref
    k_refs, v_refs = refs[:n_pages], refs[n_pages:2 * n_pages]
    out_ref, m_ref, l_ref, acc_ref = refs[2 * n_pages:]
    s_idx = pl.program_id(1)
    last = s_idx == pl.num_programs(1) - 1
    nrow = qc_ref.shape[2]
    cols = k_refs[0].shape[1] // 2
    n_near = bnear_ref.shape[0]
    nt = (((1,), (1,)), ((), ()))
    qcs = (qc_ref[0, 0], qc_ref[0, 1])

    @pl.when(s_idx == 0)
    def _():
        m_ref[...] = jnp.full(m_ref.shape, NEG_INF, F32)
        l_ref[...] = jnp.zeros(l_ref.shape, F32)
        acc_ref[...] = jnp.zeros(acc_ref.shape, F32)

    bfar = bfar_ref[...]
    bias = [bfar] * (n_pages - n_near) + [jnp.where(last, bnear_ref[i], bfar) for i in range(n_near)]
    s_maps = []
    for c in range(2):
        parts = [lax.dot_general(qcs[c], k_refs[i][0, pl.ds(c, cols, stride=2), :].astype(BF16), nt,
                                 preferred_element_type=F32) + bias[i] for i in range(n_pages)]
        s_maps.append(jnp.concatenate(parts, axis=1))
    s_past = jnp.concatenate(s_maps, axis=0)

    def pv_past(p):
        out = jnp.dot(p[:, 0:cols], v_refs[0][0].astype(BF16), preferred_element_type=F32)
        for i in range(1, n_pages):
            out = out + jnp.dot(p[:, i * cols:(i + 1) * cols], v_refs[i][0].astype(BF16), preferred_element_type=F32)
        return out

    _softmax_update(s_past, _block_max(s_past), m_ref, l_ref, acc_ref, pv_past)

    @pl.when(last)
    def _():
        s_new = jnp.concatenate([lax.dot_general(qcs[c], knew_ref[0, c], nt, preferred_element_type=F32) + bnew_ref[...]
                                 for c in range(2)], axis=0)
        row = lax.broadcasted_iota(jnp.int32, s_new.shape, 0)
        col = lax.broadcasted_iota(jnp.int32, s_new.shape, 1)
        valid = (col % N_HEADS == (row % nrow) // t_new) & (col // N_HEADS <= row % t_new)
        s_new = jnp.where(valid, s_new, NEG_INF)
        _softmax_update(s_new, _block_max(s_new), m_ref, l_ref, acc_ref,
                        lambda p: jnp.dot(p, vnew_ref[0], preferred_element_type=F32))

        lam = _lambda_value(lq1_ref, lk1_ref, lq2_ref, lk2_ref)
        o = acc_ref[...] / jnp.sum(l_ref[...], axis=-1, keepdims=True)
        out_ref[0] = _subln_gate(o[0:nrow] - lam * o[nrow:2 * nrow], sub_ref[...], szb_ref[0])


def _decode_bias(bdl, past, page, tn):
    nrow = N_HEADS * tn
    same_head = jnp.eye(N_HEADS, dtype=bool)[:, None, None, :]
    n_far = (past - MAX_DISTANCE + 1) // page
    n_near = past // page - n_far
    far = jnp.broadcast_to(jnp.where(same_head, bdl[:, MAX_DISTANCE][:, None, None, None], NEG_INF),
                           (N_HEADS, tn, page, N_HEADS)).reshape(nrow, page * N_HEADS)
    span = n_near * page
    near = jnp.stack([jnp.flip(bdl[:, qi + 1:qi + 1 + span], axis=1) for qi in range(tn)], axis=1)
    near = jnp.where(same_head, near[..., None], NEG_INF).reshape(nrow, n_near, page * N_HEADS)
    new_tok = LANES // N_HEADS
    dist_new = np.maximum(np.arange(tn)[:, None] - np.arange(new_tok)[None, :], 0)
    new = jnp.where(same_head, bdl[:, dist_new][..., None], 0.0).reshape(nrow, LANES)
    return far, jnp.transpose(near, (1, 0, 2)), new


def _attn_decode(page_table, cache_kr, cache_vr, qc, bias_far, bias_near, k_new, v_new, bias_new, szb, sub,
                 lq1, lk1, lq2, lk2):
    nb, n_pt = page_table.shape
    _, k_rows, _ = cache_kr.shape
    _, v_rows, _ = cache_vr.shape
    nrow = qc.shape[2]
    t_new = nrow // N_HEADS
    npg = DEC_PAGES
    assert n_pt % npg == 0 and k_rows == 2 * v_rows and bias_near.shape[0] <= npg
    new_rows = k_new.shape[2]

    def page_spec(rows, width, i):
        return pl.BlockSpec((1, rows, width), lambda b, s, pt, i=i: (pt[b, s * npg + i], 0, 0))

    vec = pl.BlockSpec((1, HEAD_DIM), lambda b, s, pt: (0, 0))
    in_specs = [
        pl.BlockSpec((1, 2, nrow, HEAD_DIM), lambda b, s, pt: (b, 0, 0, 0)),
        pl.BlockSpec(bias_far.shape, lambda b, s, pt: (0, 0)),
        pl.BlockSpec(bias_near.shape, lambda b, s, pt: (0, 0, 0)),
        pl.BlockSpec((1, 2, new_rows, HEAD_DIM), lambda b, s, pt: (b, 0, 0, 0)),
        pl.BlockSpec((1, new_rows, V_DIM), lambda b, s, pt: (b, 0, 0)),
        pl.BlockSpec((nrow, new_rows), lambda b, s, pt: (0, 0)),
        pl.BlockSpec((1, nrow, V_DIM), lambda b, s, pt: (b, 0, 0)),
        pl.BlockSpec((1, V_DIM), lambda b, s, pt: (0, 0)),
        vec, vec, vec, vec,
    ] + [page_spec(k_rows, HEAD_DIM, i) for i in range(npg)] + [page_spec(v_rows, V_DIM, i) for i in range(npg)]
    page_bytes = k_rows * HEAD_DIM * 4
    vmem = 2 * 2 * npg * page_bytes + 10 * 2 * nrow * npg * v_rows * 4
    return pl.pallas_call(
        functools.partial(_attn_decode_kernel, n_pages=npg, t_new=t_new),
        grid_spec=pltpu.PrefetchScalarGridSpec(
            num_scalar_prefetch=1,
            grid=(nb, n_pt // npg),
            in_specs=in_specs,
            out_specs=pl.BlockSpec((1, nrow, V_DIM), lambda b, s, pt: (b, 0, 0)),
            scratch_shapes=[pltpu.VMEM((2 * nrow, LANES), F32), pltpu.VMEM((2 * nrow, LANES), F32),
                            pltpu.VMEM((2 * nrow, V_DIM), F32)],
        ),
        out_shape=jax.ShapeDtypeStruct((nb, nrow, V_DIM), BF16),
        compiler_params=_cparams(("arbitrary", "arbitrary"), vmem // MIB + 6),
        name="attn_decode",
    )(page_table, qc, bias_far, bias_near, k_new, v_new, bias_new, szb, sub, lq1, lk1, lq2, lk2,
      *([cache_kr] * npg), *([cache_vr] * npg))


def _final_kernel(hb_ref, ma_ref, sgb_ref, x_ref, p_ref, wpb_ref, wout_ref, wgate_ref, wple_ref, lng_ref, lnb_ref, y_ref):
    b_proj = jnp.dot(hb_ref[...], wpb_ref[...], preferred_element_type=F32)
    merged = ma_ref[...] + sgb_ref[...] * b_proj
    z = ALPHA * x_ref[...] + jnp.dot(merged.astype(BF16), wout_ref[...], preferred_element_type=F32)
    mu = jnp.mean(z, axis=-1, keepdims=True)
    zc = z - mu
    var = jnp.mean(zc * zc, axis=-1, keepdims=True)
    h = zc * lax.rsqrt(var + LN_EPS) * lng_ref[...] + lnb_ref[...]
    gate = jax.nn.sigmoid(jnp.dot(h.astype(BF16), wgate_ref[...], preferred_element_type=F32))
    pe = jnp.dot(p_ref[...].astype(BF16), wple_ref[...], preferred_element_type=F32)
    y_ref[...] = h + gate * pe


def _final(hb, ma, sgb, sgb_block, x, p, wpb_bf, wout_bf, wgate_bf, wple_bf, lng, lnb, tm):
    m, d = x.shape
    pd = p.shape[1]
    row = lambda w: pl.BlockSpec((tm, w), lambda i: (i, 0))
    const = lambda r, c: pl.BlockSpec((r, c), lambda i: (0, 0), pipeline_mode=pl.Buffered(1))
    vmem = 2 * tm * (d * 2 + 4 * d * 4 + pd * 4) + (3 * d * d + pd * d) * 2 + 6 * tm * d * 4
    return pl.pallas_call(
        _final_kernel,
        grid=(m // tm,),
        in_specs=[row(d), row(d), pl.BlockSpec((tm, d), lambda i: (i, sgb_block)), row(d), row(pd),
                  const(d, d), const(d, d), const(d, d), const(pd, d), const(1, d), const(1, d)],
        out_specs=row(d),
        out_shape=jax.ShapeDtypeStruct((m, d), F32),
        compiler_params=_cparams(("arbitrary",), vmem // MIB + 6),
        name=f"final_m{m}",
    )(hb, ma, sgb, x, p, wpb_bf, wout_bf, wgate_bf, wple_bf, lng, lnb)


def kernel(x_prompt, x_sample, p_prompt, p_sample, cache_k, cache_v, state_conv, page_table, w_in, conv_w, conv_b, conv_ln_g, conv_ln_b, w_proj_a, lambda_q1, lambda_k1, lambda_q2, lambda_k2, subln_w, w_proj_b, w_out, ln_g, ln_b, w_ple_proj, w_ple_gate, rel_bias):
    assert w_in.shape[0] == DEPTH
    b, t, d = x_prompt.shape
    nb, tn, _ = x_sample.shape
    c = conv_w.shape[-1]
    n_maps = 2 * N_HEADS
    qk_w = n_maps * HEAD_DIM
    att_w = N_HEADS * V_DIM
    pd = p_prompt.shape[-1]
    n_phys, page = cache_k.shape[1], cache_k.shape[2]
    past = page_table.shape[1] * page

    wpa_bf = w_proj_a[0].astype(BF16)
    wpb_bf = w_proj_b[0].astype(BF16)
    wout_bf = w_out[0].astype(BF16)
    wgate_bf = w_ple_gate[0].astype(BF16)
    wple_bf = w_ple_proj[0].astype(BF16)
    cw, cb = conv_w[0], conv_b[0].reshape(1, c)
    clng, clnb = conv_ln_g[0].reshape(1, c), conv_ln_b[0].reshape(1, c)
    lng, lnb = ln_g[0].reshape(1, d), ln_b[0].reshape(1, d)
    sub = subln_w[0].reshape(1, V_DIM)
    lq1, lk1, lq2, lk2 = (v[0].reshape(1, HEAD_DIM) for v in (lambda_q1, lambda_k1, lambda_q2, lambda_k2))
    bdl = _bias_by_distance(rel_bias, MAX_DISTANCE + 2 * LANES + page + tn) * LOG2E

    ms = nb * tn
    xs = x_sample.reshape(ms, d)
    w_bf, (ua_s, ug_s, sza_s, q_s, k_s, v_s, szb_s, sga_s, sgb_s) = _in_projection_small(
        xs.astype(BF16), w_in[0], c, qk_w, att_w, d)

    xp = x_prompt.reshape(b * t, d)
    g, sza, q_bf, k_p, k_bf, v_p, v_bf, szb, sgab = _in_projection(xp, w_bf, PROJ_TM, c, qk_w, att_w, d)
    g3 = g.reshape(b, t, c)
    ma = _conv_prompt(g3, jnp.zeros((b, CONV_HALO, c), F32), cw, cb, clng, clnb, sza.reshape(b, t, c),
                      sgab.reshape(b, t, 2 * d), wpa_bf)
    hb = _attn_prompt(q_bf.reshape(b, t, qk_w), k_bf.reshape(b, t, qk_w), v_bf.reshape(b, t, att_w),
                      szb.reshape(b, t, att_w), bdl, sub, lq1, lk1, lq2, lk2)
    y_p = _final(hb.reshape(b * t, att_w), ma.reshape(b * t, d), sgab, 1, xp, p_prompt[0].reshape(b * t, pd),
                 wpb_bf, wout_bf, wgate_bf, wple_bf, lng, lnb, FINAL_TM)

    q_s, k_s_bf, v_s_bf = q_s.astype(BF16), k_s.astype(BF16), v_s.astype(BF16)
    tm_major = lambda a: jnp.transpose(a.reshape(nb, tn, a.shape[-1]), (1, 0, 2))
    ma_s_tm, g_s_tm = _conv_sample(jnp.transpose(state_conv[0], (1, 0, 2)), tm_major(ua_s), tm_major(ug_s), cw, cb,
                                   clng, clnb, tm_major(sza_s).reshape(ms, c), tm_major(sga_s).reshape(ms, d), wpa_bf)
    ma_s = jnp.transpose(ma_s_tm.reshape(tn, nb, d), (1, 0, 2)).reshape(ms, d)
    g_s = jnp.transpose(g_s_tm, (1, 0, 2))

    nrow = N_HEADS * tn
    qc = jnp.transpose(q_s.reshape(nb, tn, N_HEADS, 2, HEAD_DIM), (0, 3, 2, 1, 4)).reshape(nb, 2, nrow, HEAD_DIM)
    k_new = jnp.transpose(k_s_bf.reshape(nb, tn, N_HEADS, 2, HEAD_DIM), (0, 3, 1, 2, 4)).reshape(nb, 2, nrow, HEAD_DIM)
    k_new = jnp.pad(k_new, ((0, 0), (0, 0), (0, LANES - nrow), (0, 0)))
    v_new = jnp.pad(v_s_bf.reshape(nb, nrow, V_DIM), ((0, 0), (0, LANES - nrow), (0, 0)))
    szb_hq = jnp.transpose(szb_s.reshape(nb, tn, N_HEADS, V_DIM), (0, 2, 1, 3)).reshape(nb, nrow, V_DIM)
    bias_far, bias_near, bias_new = _decode_bias(bdl, past, page, tn)
    hq = _attn_decode(page_table, cache_k.reshape(n_phys, page * n_maps, HEAD_DIM),
                      cache_v.reshape(n_phys, page * N_HEADS, V_DIM), qc, bias_far, bias_near, k_new, v_new, bias_new,
                      szb_hq, sub, lq1, lk1, lq2, lk2)
    hb_s = jnp.transpose(hq.reshape(nb, N_HEADS, tn, V_DIM), (0, 2, 1, 3)).reshape(ms, att_w)
    y_s = _final(hb_s, ma_s, sgb_s, 0, xs, p_sample[0].reshape(ms, pd),
                 wpb_bf, wout_bf, wgate_bf, wple_bf, lng, lnb, ms)

    conv_prompt = g3[:, t - (CONV_WIDTH - 1):, :]
    conv_sample = jnp.concatenate([state_conv[0][:, tn:, :], g_s], axis=1)
    return (y_p.reshape(b, t, d), y_s.reshape(nb, tn, d),
            k_p.reshape(1, b, t, n_maps, HEAD_DIM), v_p.reshape(1, b, t, N_HEADS, V_DIM), conv_prompt[None],
            k_s.reshape(1, nb, tn, n_maps, HEAD_DIM), v_s.reshape(1, nb, tn, N_HEADS, V_DIM), conv_sample[None])
```

```python
import functools
import math

import numpy as np
import jax
import jax.numpy as jnp
from jax import lax
from jax.experimental import pallas as pl
from jax.experimental.pallas import tpu as pltpu

F32 = jnp.float32
BF16 = jnp.bfloat16

N_HEADS = 8
HEAD_DIM = 128
V_DIM = 2 * HEAD_DIM
CONV_WIDTH = 31
N_BUCKETS = 32
MAX_DISTANCE = 128
LN_EPS = 1e-5
NEG_INF = -1e30
DEPTH = 1
ALPHA = (2 * DEPTH) ** 0.25
LAM_INIT = 0.8 - 0.6 * math.exp(-0.3 * 0)
QK_SCALE = HEAD_DIM ** -0.5
LOG2E = math.log2(math.e)

V7X_VMEM_BYTES = 64 * 1024 * 1024
V7X_SUBLANES = 8
LANES = 128
MIB = 1024 * 1024

PROJ_TN = 1024
PROJ_TM = 1024
GLU_TM = 512
ATT_TQ = 512
ATT_TK = 512
DEC_PAGES = 8
FINAL_TM = 256


def _cparams(sem, vmem_mib):
    limit = min(vmem_mib * MIB, V7X_VMEM_BYTES - 6 * MIB)
    return pltpu.CompilerParams(dimension_semantics=sem, vmem_limit_bytes=limit)


def _proj_kernel(x_ref, *refs, n_w, epilogue):
    w_refs, out_refs = refs[:n_w], refs[n_w:]
    x = x_ref[...]
    accs = [jnp.dot(x, w[...], preferred_element_type=F32) for w in w_refs]
    for o_ref, o in zip(out_refs, epilogue(*accs)):
        o_ref[...] = o.astype(o_ref.dtype)


def _proj(name, x_bf, w_bf, col_starts, ncols, epilogue, out_dtypes, tm):
    m, k = x_bf.shape
    tn = PROJ_TN
    n_w = len(col_starts)
    in_specs = [pl.BlockSpec((tm, k), lambda j, i: (i, 0))]
    for c0 in col_starts:
        assert c0 % tn == 0
        in_specs.append(pl.BlockSpec((k, tn), lambda j, i, c0=c0: (0, c0 // tn + j)))
    out_specs = [pl.BlockSpec((tm, tn), lambda j, i: (i, j)) for _ in out_dtypes]
    out_shape = [jax.ShapeDtypeStruct((m, ncols), dt) for dt in out_dtypes]
    vmem = 2 * (tm * k * 2 + n_w * k * tn * 2 + sum(tm * tn * jnp.dtype(d).itemsize for d in out_dtypes))
    vmem += n_w * tm * tn * 4
    return pl.pallas_call(
        functools.partial(_proj_kernel, n_w=n_w, epilogue=epilogue),
        grid=(ncols // tn, m // tm),
        in_specs=in_specs, out_specs=out_specs, out_shape=out_shape,
        compiler_params=_cparams(("arbitrary", "arbitrary"), vmem // MIB + 6),
        name=f"proj_{name}_m{m}",
    )(x_bf, *([w_bf] * n_w))


def _ep_silu(z):
    return (jax.nn.silu(z),)


def _ep_sigmoid(z):
    return (jax.nn.sigmoid(z),)


def _ep_query(z):
    return (z * (QK_SCALE * LOG2E),)


def _ep_copy2(z):
    return (z, z)


def _glu_cast_kernel(x_ref, wa_ref, wg_ref, g_ref, xbf_ref):
    xb = x_ref[...].astype(BF16)
    xbf_ref[...] = xb
    ua = jnp.dot(xb, wa_ref[...], preferred_element_type=F32)
    ug = jnp.dot(xb, wg_ref[...], preferred_element_type=F32)
    g_ref[...] = ua * jax.nn.sigmoid(ug)


def _proj_glu_cast(x_f32, w_bf, c_conv):
    m, k = x_f32.shape
    tm, tn = GLU_TM, PROJ_TN
    assert c_conv == tn and m % tm == 0
    vmem = 2 * (tm * k * 4 + 2 * k * tn * 2 + tm * tn * 4 + tm * k * 2) + 2 * tm * tn * 4 + tm * k * 2
    return pl.pallas_call(
        _glu_cast_kernel,
        grid=(m // tm,),
        in_specs=[pl.BlockSpec((tm, k), lambda i: (i, 0)), pl.BlockSpec((k, tn), lambda i: (0, 0)),
                  pl.BlockSpec((k, tn), lambda i: (0, 1))],
        out_specs=[pl.BlockSpec((tm, tn), lambda i: (i, 0)), pl.BlockSpec((tm, k), lambda i: (i, 0))],
        out_shape=[jax.ShapeDtypeStruct((m, c_conv), F32), jax.ShapeDtypeStruct((m, k), BF16)],
        compiler_params=_cparams(("arbitrary",), vmem // MIB + 6),
        name="proj_glu_cast",
    )(x_f32, w_bf, w_bf)


def _in_projection(x_f32, w_bf, tm, c_conv, qk_w, att_w, d_model):
    g, x_bf = _proj_glu_cast(x_f32, w_bf, c_conv)
    o = 2 * c_conv
    (sza,) = _proj("za", x_bf, w_bf, (o,), c_conv, _ep_silu, (F32,), tm)
    o += c_conv
    (q_bf,) = _proj("q", x_bf, w_bf, (o,), qk_w, _ep_query, (BF16,), tm)
    o += qk_w
    k, k_bf = _proj("k", x_bf, w_bf, (o,), qk_w, _ep_copy2, (F32, BF16), tm)
    o += qk_w
    v, v_bf = _proj("v", x_bf, w_bf, (o,), att_w, _ep_copy2, (F32, BF16), tm)
    o += att_w
    (szb,) = _proj("zb", x_bf, w_bf, (o,), att_w, _ep_silu, (F32,), tm)
    o += att_w
    (sgb,) = _proj("gb", x_bf, w_bf, (o + d_model,), d_model, _ep_sigmoid, (F32,), tm)
    return x_bf, g, sza, q_bf, k, k_bf, v, v_bf, szb, sgb, o


def _proj_ga_kernel(x_ref, w_ref, h_ref, wpa_ref, o_ref):
    gate = jax.nn.sigmoid(jnp.dot(x_ref[...], w_ref[...], preferred_element_type=F32))
    o_ref[...] = gate * jnp.dot(h_ref[...], wpa_ref[...], preferred_element_type=F32)


def _proj_ga(x_bf, w_bf, col0, h_bf, wpa_bf, tm):
    m, k = x_bf.shape
    c, d = wpa_bf.shape
    tn = PROJ_TN
    assert col0 % tn == 0 and d % tn == 0
    vmem = 2 * (tm * k * 2 + k * tn * 2 + tm * c * 2 + c * tn * 2 + tm * tn * 4) + 2 * tm * tn * 4
    return pl.pallas_call(
        _proj_ga_kernel,
        grid=(d // tn, m // tm),
        in_specs=[pl.BlockSpec((tm, k), lambda j, i: (i, 0)), pl.BlockSpec((k, tn), lambda j, i: (0, col0 // tn + j)),
                  pl.BlockSpec((tm, c), lambda j, i: (i, 0)), pl.BlockSpec((c, tn), lambda j, i: (0, j))],
        out_specs=pl.BlockSpec((tm, tn), lambda j, i: (i, j)),
        out_shape=jax.ShapeDtypeStruct((m, d), F32),
        compiler_params=_cparams(("arbitrary", "arbitrary"), vmem // MIB + 6),
        name="proj_ga",
    )(x_bf, w_bf, h_bf, wpa_bf)


def _proj_small_kernel(x_ref, w_ref, o_ref, wbf_ref, *, silu_tiles, sigmoid_tiles, query_tiles):
    j = pl.program_id(0)
    in_range = lambda r: (j >= r[0]) & (j < r[1])
    any_range = lambda rs: functools.reduce(jnp.logical_or, [in_range(r) for r in rs])
    w = w_ref[...].astype(BF16)
    wbf_ref[...] = w
    z = jnp.dot(x_ref[...], w, preferred_element_type=F32)
    sg = jax.nn.sigmoid(z)
    o_ref[...] = jnp.where(any_range(silu_tiles), z * sg,
                           jnp.where(any_range(sigmoid_tiles), sg,
                                     jnp.where(any_range(query_tiles), z * (QK_SCALE * LOG2E), z)))


def _in_projection_small(x_bf, w_f32, c_conv, qk_w, att_w, d_model):
    m, k = x_bf.shape
    tn = PROJ_TN
    sizes = (c_conv, c_conv, c_conv, qk_w, qk_w, att_w, att_w, d_model, d_model)
    starts = np.concatenate([[0], np.cumsum(sizes)])
    assert all(s % tn == 0 for s in sizes)
    tiles = lambda i: (int(starts[i]) // tn, int(starts[i + 1]) // tn)
    n_in = int(starts[-1])
    xin, w_bf = pl.pallas_call(
        functools.partial(_proj_small_kernel, silu_tiles=(tiles(2), tiles(6)), sigmoid_tiles=(tiles(7), tiles(8)),
                          query_tiles=(tiles(3),)),
        grid=(n_in // tn,),
        in_specs=[pl.BlockSpec((m, k), lambda j: (0, 0)), pl.BlockSpec((k, tn), lambda j: (0, j))],
        out_specs=[pl.BlockSpec((m, tn), lambda j: (0, j)), pl.BlockSpec((k, tn), lambda j: (0, j))],
        out_shape=[jax.ShapeDtypeStruct((m, n_in), F32), jax.ShapeDtypeStruct((k, n_in), BF16)],
        compiler_params=_cparams(("arbitrary",), (2 * (m * k * 2 + k * tn * 6 + m * tn * 4) + k * tn * 2 + 4 * m * tn * 4) // MIB + 6),
        name="proj_small",
    )(x_bf, w_f32)
    return w_bf, tuple(xin[:, int(starts[i]):int(starts[i + 1])] for i in range(len(sizes)))


def _ln_swish(acc, lng, lnb):
    mu = jnp.mean(acc, axis=-1, keepdims=True)
    xc = acc - mu
    var = jnp.mean(xc * xc, axis=-1, keepdims=True)
    y = xc * lax.rsqrt(var + LN_EPS) * lng + lnb
    return y * jax.nn.sigmoid(y)


def _conv_sample_kernel(state_ref, ua_ref, ug_ref, cw_ref, cb_ref, lng_ref, lnb_ref, sza_ref, sga_ref, wpa_ref,
                        out_ref, g_ref, hs_ref):
    n_state, nb, _ = state_ref.shape
    t_new = ua_ref.shape[0]
    g_ref[...] = ua_ref[...] * jax.nn.sigmoid(ug_ref[...])
    window = lambda r: state_ref[r] if r < n_state else g_ref[r - n_state]
    for t in range(t_new):
        acc = jnp.broadcast_to(cb_ref[...], (nb, cb_ref.shape[-1]))
        for j in range(CONV_WIDTH):
            acc = acc + cw_ref[j:j + 1, :] * window(t + j)
        h = _ln_swish(acc, lng_ref[...], lnb_ref[...]) * sza_ref[t * nb:(t + 1) * nb, :]
        hs_ref[t * nb:(t + 1) * nb, :] = h.astype(BF16)
    out_ref[...] = sga_ref[...] * jnp.dot(hs_ref[...], wpa_ref[...], preferred_element_type=F32)


def _conv_sample(state_tm, ua_tm, ug_tm, conv_w, conv_b, lng, lnb, sza_tm, sga_tm, wpa_bf):
    t_new, nb, c = ua_tm.shape
    d = wpa_bf.shape[1]
    assert state_tm.shape[0] == CONV_WIDTH - 1
    return pl.pallas_call(
        _conv_sample_kernel,
        out_shape=[jax.ShapeDtypeStruct((t_new * nb, d), F32), jax.ShapeDtypeStruct((t_new, nb, c), F32)],
        scratch_shapes=[pltpu.VMEM((t_new * nb, c), BF16)],
        compiler_params=pltpu.CompilerParams(vmem_limit_bytes=32 * MIB),
        name="conv_sample",
    )(state_tm, ua_tm, ug_tm, conv_w, conv_b, lng, lnb, sza_tm, sga_tm, wpa_bf)


def _lambda_value(lq1_ref, lk1_ref, lq2_ref, lk2_ref):
    s1 = jnp.sum(lq1_ref[...] * lk1_ref[...], axis=-1, keepdims=True)
    s2 = jnp.sum(lq2_ref[...] * lk2_ref[...], axis=-1, keepdims=True)
    return jnp.exp(s1) - jnp.exp(s2) + LAM_INIT


def _subln_gate(o, sub, szb):
    r = o * lax.rsqrt(jnp.mean(o * o, axis=-1, keepdims=True) + LN_EPS) * sub * (1.0 - LAM_INIT)
    return (r * szb).astype(BF16)


def _lane_blocks(x):
    return [x[:, i * LANES:(i + 1) * LANES] for i in range(x.shape[1] // LANES)]


def _block_max(s):
    return functools.reduce(jnp.maximum, _lane_blocks(s))


def _softmax_update(s, bmax, m_ref, l_ref, acc_ref, pv_fn):
    m_old = m_ref[...]
    m_new = jnp.maximum(m_old, jnp.max(bmax, axis=-1, keepdims=True))
    alpha = jnp.exp2(m_old - m_new)
    p_blocks = [jnp.exp2(sb - m_new) for sb in _lane_blocks(s)]
    l_ref[...] = alpha * l_ref[...] + functools.reduce(jnp.add, p_blocks)
    pv = pv_fn(jnp.concatenate(p_blocks, axis=1).astype(BF16))
    acc_ref[...] = jnp.concatenate([alpha] * (pv.shape[1] // LANES), axis=1) * acc_ref[...] + pv
    m_ref[...] = m_new


def _bias_by_distance(rel_bias, n_max):
    n = jnp.arange(n_max)
    max_exact = N_BUCKETS // 2
    large = max_exact + (jnp.log(jnp.maximum(n, 1).astype(F32) / max_exact)
                         / math.log(MAX_DISTANCE / max_exact) * (N_BUCKETS - max_exact)).astype(jnp.int32)
    large = jnp.minimum(large, N_BUCKETS - 1)
    bucket = jnp.where(n < max_exact, n, large)
    return jnp.transpose(rel_bias[bucket].astype(F32), (1, 0))


def _toeplitz_blocks(bdl):
    assert MAX_DISTANCE <= LANES
    per = 2 * LANES
    n = np.arange(per)
    blocks = []
    for e in range(2):
        idx = np.where(n < LANES, np.maximum(LANES * e - n, 0), LANES * e + per - n)
        w = bdl[:, idx]
        x = jnp.tile(w, (1, LANES))[:, :LANES * (per - 1)].reshape(-1, LANES, per - 1)
        blocks.append(x[:, :, :LANES])
    return jnp.stack(blocks, axis=1)


def _attn_prompt_kernel(cfar_ref, q_ref, k_ref, v_ref, u_ref, szb_ref, sub_ref, lq1_ref, lk1_ref, lq2_ref, lk2_ref,
                        out_ref, strip_ref, s_ref, bm_ref, m_ref, l_ref, acc_ref, *, tq, tk):
    h = pl.program_id(1)
    qi = pl.program_id(2)
    far = tq + 2 * LANES

    @pl.when(qi == 0)
    def _():
        strip_ref[...] = jnp.full(strip_ref.shape, cfar_ref[h], F32)
        strip_ref[tq:tq + LANES, :] = u_ref[0, 0]
        strip_ref[tq + LANES:far, :] = u_ref[0, 1]

    q = q_ref[0]
    qs = (q[:, :HEAD_DIM], q[:, HEAD_DIM:])
    m_ref[...] = jnp.full(m_ref.shape, NEG_INF, F32)
    l_ref[...] = jnp.zeros(l_ref.shape, F32)
    acc_ref[...] = jnp.zeros(acc_ref.shape, F32)

    def scores(j, slot):
        off = pl.multiple_of(j * tk, tk)
        kj = k_ref[0, pl.ds(off, tk), :]
        r0 = tq + (qi - j) * tk
        bias = [strip_ref[pl.ds(pl.multiple_of(jnp.minimum(r0 - cb * LANES, far), LANES), tq), :]
                for cb in range(tk // LANES)]
        for c in range(2):
            qk = lax.dot_general(qs[c], kj[:, c * HEAD_DIM:(c + 1) * HEAD_DIM], (((1,), (1,)), ((), ())),
                                 preferred_element_type=F32)
            s_blocks = [sb + bb for sb, bb in zip(_lane_blocks(qk), bias)]
            s_ref[slot, c] = jnp.concatenate(s_blocks, axis=1)
            bm_ref[slot, c] = functools.reduce(jnp.maximum, s_blocks)

    def consume(j, slot, mask):
        off = pl.multiple_of(j * tk, tk)
        vj = v_ref[0, pl.ds(off, tk), :]
        for c in range(2):
            s = s_ref[slot, c]
            if mask is None:
                bmax = bm_ref[slot, c]
            else:
                s = jnp.where(mask, s, NEG_INF)
                bmax = _block_max(s)
            _softmax_update(s, bmax, m_ref.at[c], l_ref.at[c], acc_ref.at[c],
                            lambda p: jnp.dot(p, vj, preferred_element_type=F32))

    scores(0, 0)

    def body(j, carry):
        consume(j, j % 2, None)
        scores(j + 1, (j + 1) % 2)
        return carry

    lax.fori_loop(0, qi, body, 0)

    row = lax.broadcasted_iota(jnp.int32, (tq, tk), 0)
    col = lax.broadcasted_iota(jnp.int32, (tq, tk), 1)
    consume(qi, qi % 2, col <= row)

    lam = _lambda_value(lq1_ref, lk1_ref, lq2_ref, lk2_ref)
    o1 = acc_ref[0] / jnp.sum(l_ref[0], axis=-1, keepdims=True)
    o2 = acc_ref[1] / jnp.sum(l_ref[1], axis=-1, keepdims=True)
    out_ref[0] = _subln_gate(o1 - lam * o2, sub_ref[...], szb_ref[0])


def _attn_prompt(q_bf, k_bf, v_bf, szb, bdl, sub, lq1, lk1, lq2, lk2):
    b, t, _ = q_bf.shape
    tq, tk = ATT_TQ, ATT_TK
    assert tq == tk and tq % LANES == 0 and t % tq == 0
    u = _toeplitz_blocks(bdl)
    cfar = bdl[:, MAX_DISTANCE]
    smem = pl.BlockSpec(memory_space=pltpu.SMEM)
    vec = pl.BlockSpec((1, HEAD_DIM), lambda bi, h, i: (0, 0))
    strip_rows = 2 * tq + 2 * LANES
    vmem = (2 * (2 * t * V_DIM * 2 + 2 * LANES * LANES * 4 + tq * V_DIM * (2 + 4 + 2))
            + 2 * 2 * tq * tk * 4 + (strip_rows + (2 * 2 + 2 * 2) * tq) * LANES * 4 + 2 * tq * V_DIM * 4
            + 4 * tq * tk * 4)
    return pl.pallas_call(
        functools.partial(_attn_prompt_kernel, tq=tq, tk=tk),
        grid=(b, N_HEADS, t // tq),
        in_specs=[
            smem,
            pl.BlockSpec((1, tq, V_DIM), lambda bi, h, i: (bi, i, h)),
            pl.BlockSpec((1, t, V_DIM), lambda bi, h, i: (bi, 0, h)),
            pl.BlockSpec((1, t, V_DIM), lambda bi, h, i: (bi, 0, h)),
            pl.BlockSpec((1, 2, LANES, LANES), lambda bi, h, i: (h, 0, 0, 0)),
            pl.BlockSpec((1, tq, V_DIM), lambda bi, h, i: (bi, i, h)),
            pl.BlockSpec((1, V_DIM), lambda bi, h, i: (0, 0)),
            vec, vec, vec, vec,
        ],
        out_specs=pl.BlockSpec((1, tq, V_DIM), lambda bi, h, i: (bi, i, h)),
        out_shape=jax.ShapeDtypeStruct((b, t, N_HEADS * V_DIM), BF16),
        scratch_shapes=[pltpu.VMEM((strip_rows, LANES), F32), pltpu.VMEM((2, 2, tq, tk), F32),
                        pltpu.VMEM((2, 2, tq, LANES), F32), pltpu.VMEM((2, tq, LANES), F32),
                        pltpu.VMEM((2, tq, LANES), F32), pltpu.VMEM((2, tq, V_DIM), F32)],
        compiler_params=_cparams(("arbitrary", "arbitrary", "arbitrary"), vmem // MIB + 6),
        name="attn_prompt",
    )(cfar, q_bf, k_bf, v_bf, u, szb, sub, lq1, lk1, lq2, lk2)


def _conv_rows(hist, cur, xs_ref, cw_ref, cb_ref, lng_ref, lnb_ref, sza):
    rb = cur.shape[0]
    n = 2 * rb
    xs_ref[0, 0:rb, :] = hist
    xs_ref[0, rb:n, :] = cur
    for r in range(1, V7X_SUBLANES):
        xs_ref[r, 0:n - V7X_SUBLANES, :] = xs_ref[0, r:r + n - V7X_SUBLANES, :]
    first = rb - (CONV_WIDTH - 1)
    acc = jnp.broadcast_to(cb_ref[...], cur.shape)
    for j in range(CONV_WIDTH):
        r, a = (first + j) % V7X_SUBLANES, (first + j) // V7X_SUBLANES
        acc = acc + cw_ref[j:j + 1, :] * xs_ref[r, V7X_SUBLANES * a:V7X_SUBLANES * a + rb, :]
    return (_ln_swish(acc, lng_ref[...], lnb_ref[...]) * sza).astype(BF16)


def _attn_decode_kernel(pt_ref, qc_ref, bfar_ref, bnear_ref, knew_ref, vnew_ref, bnew_ref, szb_ref, sub_ref,
                        lq1_ref, lk1_ref, lq2_ref, lk2_ref, gcur_ref, gprev_ref, sza_ref, cw_ref, cb_ref, lng_ref, lnb_ref,
                        *refs, n_pages, t_new, conv_blocks_per_seq):
    del pt_ref
    k_refs, v_refs = refs[:n_pages], refs[n_pages:2 * n_pages]
    out_ref, h_ref, m_ref, l_ref, acc_ref, xs_ref = refs[2 * n_pages:]
    s_idx = pl.program_id(1)

    blk = pl.program_id(0) * pl.num_programs(1) + s_idx
    hist = jnp.where(blk % conv_blocks_per_seq == 0, 0.0, gprev_ref[0])
    h_ref[0] = _conv_rows(hist, gcur_ref[0], xs_ref, cw_ref, cb_ref, lng_ref, lnb_ref, sza_ref[0])

    last = s_idx == pl.num_programs(1) - 1
    nrow = qc_ref.shape[2]
    cols = k_refs[0].shape[1] // 2
    n_near = bnear_ref.shape[0]
    nt = (((1,), (1,)), ((), ()))
    qcs = (qc_ref[0, 0], qc_ref[0, 1])

    @pl.when(s_idx == 0)
    def _():
        m_ref[...] = jnp.full(m_ref.shape, NEG_INF, F32)
        l_ref[...] = jnp.zeros(l_ref.shape, F32)
        acc_ref[...] = jnp.zeros(acc_ref.shape, F32)

    bfar = bfar_ref[...]
    bias = [bfar] * (n_pages - n_near) + [jnp.where(last, bnear_ref[i], bfar) for i in range(n_near)]
    s_maps = []
    for c in range(2):
        parts = [lax.dot_general(qcs[c], k_refs[i][0, pl.ds(c, cols, stride=2), :].astype(BF16), nt,
                                 preferred_element_type=F32) + bias[i] for i in range(n_pages)]
        s_maps.append(jnp.concatenate(parts, axis=1))
    s_past = jnp.concatenate(s_maps, axis=0)

    def pv_past(p):
        out = jnp.dot(p[:, 0:cols], v_refs[0][0].astype(BF16), preferred_element_type=F32)
        for i in range(1, n_pages):
            out = out + jnp.dot(p[:, i * cols:(i + 1) * cols], v_refs[i][0].astype(BF16), preferred_element_type=F32)
        return out

    _softmax_update(s_past, _block_max(s_past), m_ref, l_ref, acc_ref, pv_past)

    @pl.when(last)
    def _():
        s_new = jnp.concatenate([lax.dot_general(qcs[c], knew_ref[0, c], nt, preferred_element_type=F32) + bnew_ref[...]
                                 for c in range(2)], axis=0)
        row = lax.broadcasted_iota(jnp.int32, s_new.shape, 0)
        col = lax.broadcasted_iota(jnp.int32, s_new.shape, 1)
        valid = (col % N_HEADS == (row % nrow) // t_new) & (col // N_HEADS <= row % t_new)
        s_new = jnp.where(valid, s_new, NEG_INF)
        _softmax_update(s_new, _block_max(s_new), m_ref, l_ref, acc_ref,
                        lambda p: jnp.dot(p, vnew_ref[0], preferred_element_type=F32))

        lam = _lambda_value(lq1_ref, lk1_ref, lq2_ref, lk2_ref)
        o = acc_ref[...] / jnp.sum(l_ref[...], axis=-1, keepdims=True)
        out_ref[0] = _subln_gate(o[0:nrow] - lam * o[nrow:2 * nrow], sub_ref[...], szb_ref[0])


def _decode_bias(bdl, past, page, tn):
    nrow = N_HEADS * tn
    same_head = jnp.eye(N_HEADS, dtype=bool)[:, None, None, :]
    n_far = (past - MAX_DISTANCE + 1) // page
    n_near = past // page - n_far
    far = jnp.broadcast_to(jnp.where(same_head, bdl[:, MAX_DISTANCE][:, None, None, None], NEG_INF),
                           (N_HEADS, tn, page, N_HEADS)).reshape(nrow, page * N_HEADS)
    span = n_near * page
    near = jnp.stack([jnp.flip(bdl[:, qi + 1:qi + 1 + span], axis=1) for qi in range(tn)], axis=1)
    near = jnp.where(same_head, near[..., None], NEG_INF).reshape(nrow, n_near, page * N_HEADS)
    new_tok = LANES // N_HEADS
    dist_new = np.maximum(np.arange(tn)[:, None] - np.arange(new_tok)[None, :], 0)
    new = jnp.where(same_head, bdl[:, dist_new][..., None], 0.0).reshape(nrow, LANES)
    return far, jnp.transpose(near, (1, 0, 2)), new


def _attn_decode(page_table, cache_kr, cache_vr, qc, bias_far, bias_near, k_new, v_new, bias_new, szb, sub,
                 lq1, lk1, lq2, lk2, g, sza, conv_w, conv_b, clng, clnb):
    nb, n_pt = page_table.shape
    _, k_rows, _ = cache_kr.shape
    _, v_rows, _ = cache_vr.shape
    nrow = qc.shape[2]
    t_new = nrow // N_HEADS
    npg = DEC_PAGES
    assert n_pt % npg == 0 and k_rows == 2 * v_rows and bias_near.shape[0] <= npg
    new_rows = k_new.shape[2]

    def page_spec(rows, width, i):
        return pl.BlockSpec((1, rows, width), lambda b, s, pt, i=i: (pt[b, s * npg + i], 0, 0))

    ns = n_pt // npg
    bp, t, c = g.shape
    rb = bp * t // (nb * ns)
    assert rb * nb * ns == bp * t and rb % V7X_SUBLANES == 0 and rb >= CONV_WIDTH - 1 and t % rb == 0
    cps = t // rb
    cblk = lambda b, s, pt: ((b * ns + s) // cps, (b * ns + s) % cps, 0)
    cprev = lambda b, s, pt: ((b * ns + s) // cps, jnp.maximum((b * ns + s) % cps - 1, 0), 0)
    cconst = lambda shape: pl.BlockSpec(shape, lambda b, s, pt: (0,) * len(shape))
    vec = pl.BlockSpec((1, HEAD_DIM), lambda b, s, pt: (0, 0))
    in_specs = [
        pl.BlockSpec((1, 2, nrow, HEAD_DIM), lambda b, s, pt: (b, 0, 0, 0)),
        pl.BlockSpec(bias_far.shape, lambda b, s, pt: (0, 0)),
        pl.BlockSpec(bias_near.shape, lambda b, s, pt: (0, 0, 0)),
        pl.BlockSpec((1, 2, new_rows, HEAD_DIM), lambda b, s, pt: (b, 0, 0, 0)),
        pl.BlockSpec((1, new_rows, V_DIM), lambda b, s, pt: (b, 0, 0)),
        pl.BlockSpec((nrow, new_rows), lambda b, s, pt: (0, 0)),
        pl.BlockSpec((1, nrow, V_DIM), lambda b, s, pt: (b, 0, 0)),
        pl.BlockSpec((1, V_DIM), lambda b, s, pt: (0, 0)),
        vec, vec, vec, vec,
        pl.BlockSpec((1, rb, c), cblk), pl.BlockSpec((1, rb, c), cprev), pl.BlockSpec((1, rb, c), cblk),
        cconst((CONV_WIDTH, c)), cconst((1, c)), cconst((1, c)), cconst((1, c)),
    ] + [page_spec(k_rows, HEAD_DIM, i) for i in range(npg)] + [page_spec(v_rows, V_DIM, i) for i in range(npg)]
    page_bytes = k_rows * HEAD_DIM * 4
    vmem = 2 * 2 * npg * page_bytes + 10 * 2 * nrow * npg * v_rows * 4 + 24 * rb * c * 4
    return pl.pallas_call(
        functools.partial(_attn_decode_kernel, n_pages=npg, t_new=t_new, conv_blocks_per_seq=cps),
        grid_spec=pltpu.PrefetchScalarGridSpec(
            num_scalar_prefetch=1,
            grid=(nb, ns),
            in_specs=in_specs,
            out_specs=[pl.BlockSpec((1, nrow, V_DIM), lambda b, s, pt: (b, 0, 0)), pl.BlockSpec((1, rb, c), cblk)],
            scratch_shapes=[pltpu.VMEM((2 * nrow, LANES), F32), pltpu.VMEM((2 * nrow, LANES), F32),
                            pltpu.VMEM((2 * nrow, V_DIM), F32), pltpu.VMEM((V7X_SUBLANES, 2 * rb, c), F32)],
        ),
        out_shape=[jax.ShapeDtypeStruct((nb, nrow, V_DIM), BF16), jax.ShapeDtypeStruct((bp, t, c), BF16)],
        compiler_params=_cparams(("arbitrary", "arbitrary"), vmem // MIB + 6),
        name="attn_decode",
    )(page_table, qc, bias_far, bias_near, k_new, v_new, bias_new, szb, sub, lq1, lk1, lq2, lk2,
      g, g, sza, conv_w, conv_b, clng, clnb,
      *([cache_kr] * npg), *([cache_vr] * npg))


def _final_kernel(hb_ref, ma_ref, sgb_ref, x_ref, p_ref, wpb_ref, wout_ref, wgate_ref, wple_ref, lng_ref, lnb_ref, y_ref):
    b_proj = jnp.dot(hb_ref[...], wpb_ref[...], preferred_element_type=F32)
    merged = ma_ref[...] + sgb_ref[...] * b_proj
    z = ALPHA * x_ref[...] + jnp.dot(merged.astype(BF16), wout_ref[...], preferred_element_type=F32)
    mu = jnp.mean(z, axis=-1, keepdims=True)
    zc = z - mu
    var = jnp.mean(zc * zc, axis=-1, keepdims=True)
    h = zc * lax.rsqrt(var + LN_EPS) * lng_ref[...] + lnb_ref[...]
    gate = jax.nn.sigmoid(jnp.dot(h.astype(BF16), wgate_ref[...], preferred_element_type=F32))
    pe = jnp.dot(p_ref[...].astype(BF16), wple_ref[...], preferred_element_type=F32)
    y_ref[...] = h + gate * pe


def _final(hb, ma, sgb, sgb_block, x, p, wpb_bf, wout_bf, wgate_bf, wple_bf, lng, lnb, tm):
    m, d = x.shape
    pd = p.shape[1]
    row = lambda w: pl.BlockSpec((tm, w), lambda i: (i, 0))
    const = lambda r, c: pl.BlockSpec((r, c), lambda i: (0, 0), pipeline_mode=pl.Buffered(1))
    vmem = 2 * tm * (d * 2 + 4 * d * 4 + pd * 4) + (3 * d * d + pd * d) * 2 + 6 * tm * d * 4
    return pl.pallas_call(
        _final_kernel,
        grid=(m // tm,),
        in_specs=[row(d), row(d), pl.BlockSpec((tm, d), lambda i: (i, sgb_block)), row(d), row(pd),
                  const(d, d), const(d, d), const(d, d), const(pd, d), const(1, d), const(1, d)],
        out_specs=row(d),
        out_shape=jax.ShapeDtypeStruct((m, d), F32),
        compiler_params=_cparams(("arbitrary",), vmem // MIB + 6),
        name=f"final_m{m}",
    )(hb, ma, sgb, x, p, wpb_bf, wout_bf, wgate_bf, wple_bf, lng, lnb)


def kernel(x_prompt, x_sample, p_prompt, p_sample, cache_k, cache_v, state_conv, page_table, w_in, conv_w, conv_b, conv_ln_g, conv_ln_b, w_proj_a, lambda_q1, lambda_k1, lambda_q2, lambda_k2, subln_w, w_proj_b, w_out, ln_g, ln_b, w_ple_proj, w_ple_gate, rel_bias):
    assert w_in.shape[0] == DEPTH
    b, t, d = x_prompt.shape
    nb, tn, _ = x_sample.shape
    c = conv_w.shape[-1]
    n_maps = 2 * N_HEADS
    qk_w = n_maps * HEAD_DIM
    att_w = N_HEADS * V_DIM
    pd = p_prompt.shape[-1]
    n_phys, page = cache_k.shape[1], cache_k.shape[2]
    past = page_table.shape[1] * page

    wpa_bf = w_proj_a[0].astype(BF16)
    wpb_bf = w_proj_b[0].astype(BF16)
    wout_bf = w_out[0].astype(BF16)
    wgate_bf = w_ple_gate[0].astype(BF16)
    wple_bf = w_ple_proj[0].astype(BF16)
    cw, cb = conv_w[0], conv_b[0].reshape(1, c)
    clng, clnb = conv_ln_g[0].reshape(1, c), conv_ln_b[0].reshape(1, c)
    lng, lnb = ln_g[0].reshape(1, d), ln_b[0].reshape(1, d)
    sub = subln_w[0].reshape(1, V_DIM)
    lq1, lk1, lq2, lk2 = (v[0].reshape(1, HEAD_DIM) for v in (lambda_q1, lambda_k1, lambda_q2, lambda_k2))
    bdl = _bias_by_distance(rel_bias, MAX_DISTANCE + 2 * LANES + page + tn) * LOG2E

    ms = nb * tn
    xs = x_sample.reshape(ms, d)
    w_bf, (ua_s, ug_s, sza_s, q_s, k_s, v_s, szb_s, sga_s, sgb_s) = _in_projection_small(
        xs.astype(BF16), w_in[0], c, qk_w, att_w, d)

    xp = x_prompt.reshape(b * t, d)
    xp_bf, g, sza, q_bf, k_p, k_bf, v_p, v_bf, szb, sgb, ga_col0 = _in_projection(xp, w_bf, PROJ_TM, c, qk_w, att_w, d)
    g3 = g.reshape(b, t, c)
    hb = _attn_prompt(q_bf.reshape(b, t, qk_w), k_bf.reshape(b, t, qk_w), v_bf.reshape(b, t, att_w),
                      szb.reshape(b, t, att_w), bdl, sub, lq1, lk1, lq2, lk2)

    q_s, k_s_bf, v_s_bf = q_s.astype(BF16), k_s.astype(BF16), v_s.astype(BF16)
    tm_major = lambda a: jnp.transpose(a.reshape(nb, tn, a.shape[-1]), (1, 0, 2))
    ma_s_tm, g_s_tm = _conv_sample(jnp.transpose(state_conv[0], (1, 0, 2)), tm_major(ua_s), tm_major(ug_s), cw, cb,
                                   clng, clnb, tm_major(sza_s).reshape(ms, c), tm_major(sga_s).reshape(ms, d), wpa_bf)
    ma_s = jnp.transpose(ma_s_tm.reshape(tn, nb, d), (1, 0, 2)).reshape(ms, d)
    g_s = jnp.transpose(g_s_tm, (1, 0, 2))

    nrow = N_HEADS * tn
    qc = jnp.transpose(q_s.reshape(nb, tn, N_HEADS, 2, HEAD_DIM), (0, 3, 2, 1, 4)).reshape(nb, 2, nrow, HEAD_DIM)
    k_new = jnp.transpose(k_s_bf.reshape(nb, tn, N_HEADS, 2, HEAD_DIM), (0, 3, 1, 2, 4)).reshape(nb, 2, nrow, HEAD_DIM)
    k_new = jnp.pad(k_new, ((0, 0), (0, 0), (0, LANES - nrow), (0, 0)))
    v_new = jnp.pad(v_s_bf.reshape(nb, nrow, V_DIM), ((0, 0), (0, LANES - nrow), (0, 0)))
    szb_hq = jnp.transpose(szb_s.reshape(nb, tn, N_HEADS, V_DIM), (0, 2, 1, 3)).reshape(nb, nrow, V_DIM)
    bias_far, bias_near, bias_new = _decode_bias(bdl, past, page, tn)
    hq, h_conv = _attn_decode(page_table, cache_k.reshape(n_phys, page * n_maps, HEAD_DIM),
                              cache_v.reshape(n_phys, page * N_HEADS, V_DIM), qc, bias_far, bias_near, k_new, v_new,
                              bias_new, szb_hq, sub, lq1, lk1, lq2, lk2, g3, sza.reshape(b, t, c), cw, cb, clng, clnb)
    hb_s = jnp.transpose(hq.reshape(nb, N_HEADS, tn, V_DIM), (0, 2, 1, 3)).reshape(ms, att_w)
    y_s = _final(hb_s, ma_s, sgb_s, 0, xs, p_sample[0].reshape(ms, pd),
                 wpb_bf, wout_bf, wgate_bf, wple_bf, lng, lnb, ms)

    ma = _proj_ga(xp_bf, w_bf, ga_col0, h_conv.reshape(b * t, c), wpa_bf, PROJ_TM)
    y_p = _final(hb.reshape(b * t, att_w), ma, sgb, 0, xp, p_prompt[0].reshape(b * t, pd),
                 wpb_bf, wout_bf, wgate_bf, wple_bf, lng, lnb, FINAL_TM)

    conv_prompt = g3[:, t - (CONV_WIDTH - 1):, :]
    conv_sample = jnp.concatenate([state_conv[0][:, tn:, :], g_s], axis=1)
    return (y_p.reshape(b, t, d), y_s.reshape(nb, tn, d),
            k_p.reshape(1, b, t, n_maps, HEAD_DIM), v_p.reshape(1, b, t, N_HEADS, V_DIM), conv_prompt[None],
            k_s.reshape(1, nb, tn, n_maps, HEAD_DIM), v_s.reshape(1, nb, tn, N_HEADS, V_DIM), conv_sample[None])
```

```python
import functools
import math

import numpy as np
import jax
import jax.numpy as jnp
from jax import lax
from jax.experimental import pallas as pl
from jax.experimental.pallas import tpu as pltpu

F32 = jnp.float32
BF16 = jnp.bfloat16

N_HEADS = 8
HEAD_DIM = 128
V_DIM = 2 * HEAD_DIM
CONV_WIDTH = 31
N_BUCKETS = 32
MAX_DISTANCE = 128
LN_EPS = 1e-5
NEG_INF = -1e30
DEPTH = 1
ALPHA = (2 * DEPTH) ** 0.25
LAM_INIT = 0.8 - 0.6 * math.exp(-0.3 * 0)
QK_SCALE = HEAD_DIM ** -0.5
LOG2E = math.log2(math.e)

V7X_VMEM_BYTES = 64 * 1024 * 1024
V7X_SUBLANES = 8
LANES = 128
MIB = 1024 * 1024

PROJ_TN = 1024
PROJ_TM = 1024
GLU_TM = 512
ATT_TQ = 512
ATT_TK = 512
DEC_PAGES = 8
DEC_GROUPS = 4
FINAL_TM = 256


def _cparams(sem, vmem_mib):
    limit = min(vmem_mib * MIB, V7X_VMEM_BYTES - 6 * MIB)
    return pltpu.CompilerParams(dimension_semantics=sem, vmem_limit_bytes=limit)


def _proj_kernel(x_ref, *refs, n_w, epilogue):
    w_refs, out_refs = refs[:n_w], refs[n_w:]
    x = x_ref[...]
    accs = [jnp.dot(x, w[...], preferred_element_type=F32) for w in w_refs]
    for o_ref, o in zip(out_refs, epilogue(*accs)):
        o_ref[...] = o.astype(o_ref.dtype).reshape(o_ref.shape)


def _proj(name, x_bf, w_bf, col_starts, ncols, epilogue, out_dtypes, tm, head_width=None):
    m, k = x_bf.shape
    tn = PROJ_TN
    n_w = len(col_starts)
    in_specs = [pl.BlockSpec((tm, k), lambda j, i: (i, 0))]
    for c0 in col_starts:
        assert c0 % tn == 0
        in_specs.append(pl.BlockSpec((k, tn), lambda j, i, c0=c0: (0, c0 // tn + j)))
    out_specs = [pl.BlockSpec((tm, tn), lambda j, i: (i, j)) for _ in out_dtypes]
    out_shape = [jax.ShapeDtypeStruct((m, ncols), dt) for dt in out_dtypes]
    if head_width is not None:
        assert (tn // head_width) % V7X_SUBLANES == 0
        out_specs[0] = pl.BlockSpec((tm, tn // head_width, head_width), lambda j, i: (i, j, 0))
        out_shape[0] = jax.ShapeDtypeStruct((m, ncols // head_width, head_width), out_dtypes[0])
    vmem = 2 * (tm * k * 2 + n_w * k * tn * 2 + sum(tm * tn * jnp.dtype(d).itemsize for d in out_dtypes))
    vmem += n_w * tm * tn * 4
    return pl.pallas_call(
        functools.partial(_proj_kernel, n_w=n_w, epilogue=epilogue),
        grid=(ncols // tn, m // tm),
        in_specs=in_specs, out_specs=out_specs, out_shape=out_shape,
        compiler_params=_cparams(("arbitrary", "arbitrary"), vmem // MIB + 6),
        name=f"proj_{name}_m{m}",
    )(x_bf, *([w_bf] * n_w))


def _ep_silu(z):
    return (jax.nn.silu(z),)


def _ep_sigmoid(z):
    return (jax.nn.sigmoid(z),)


def _ep_query(z):
    return (z * (QK_SCALE * LOG2E),)


def _ep_copy2(z):
    return (z, z)


def _glu_cast_kernel(x_ref, wa_ref, wg_ref, g_ref, xbf_ref):
    xb = x_ref[...].astype(BF16)
    xbf_ref[...] = xb
    ua = jnp.dot(xb, wa_ref[...], preferred_element_type=F32)
    ug = jnp.dot(xb, wg_ref[...], preferred_element_type=F32)
    g_ref[...] = ua * jax.nn.sigmoid(ug)


def _proj_glu_cast(x_f32, w_bf, c_conv):
    m, k = x_f32.shape
    tm, tn = GLU_TM, PROJ_TN
    assert c_conv == tn and m % tm == 0
    vmem = 2 * (tm * k * 4 + 2 * k * tn * 2 + tm * tn * 4 + tm * k * 2) + 2 * tm * tn * 4 + tm * k * 2
    return pl.pallas_call(
        _glu_cast_kernel,
        grid=(m // tm,),
        in_specs=[pl.BlockSpec((tm, k), lambda i: (i, 0)), pl.BlockSpec((k, tn), lambda i: (0, 0)),
                  pl.BlockSpec((k, tn), lambda i: (0, 1))],
        out_specs=[pl.BlockSpec((tm, tn), lambda i: (i, 0)), pl.BlockSpec((tm, k), lambda i: (i, 0))],
        out_shape=[jax.ShapeDtypeStruct((m, c_conv), F32), jax.ShapeDtypeStruct((m, k), BF16)],
        compiler_params=_cparams(("arbitrary",), vmem // MIB + 6),
        name="proj_glu_cast",
    )(x_f32, w_bf, w_bf)


def _proj_value_kernel(x_ref, wa_ref, wb_ref, o_ref, obf_ref):
    x = x_ref[...]
    z = jnp.concatenate([jnp.dot(x, wa_ref[...], preferred_element_type=F32),
                         jnp.dot(x, wb_ref[...], preferred_element_type=F32)], axis=1)
    o_ref[...] = z.reshape(o_ref.shape)
    obf_ref[...] = z.astype(BF16)


def _proj_value(x_bf, w_bf, col0, att_w):
    m, k = x_bf.shape
    tm, tn = GLU_TM, PROJ_TN
    assert att_w == 2 * tn == N_HEADS * V_DIM and col0 % tn == 0 and m % tm == 0
    vmem = 2 * (tm * k * 2 + 2 * k * tn * 2 + tm * att_w * 6) + 2 * tm * att_w * 4
    return pl.pallas_call(
        _proj_value_kernel,
        grid=(m // tm,),
        in_specs=[pl.BlockSpec((tm, k), lambda i: (i, 0)), pl.BlockSpec((k, tn), lambda i: (0, col0 // tn)),
                  pl.BlockSpec((k, tn), lambda i: (0, col0 // tn + 1))],
        out_specs=[pl.BlockSpec((tm, N_HEADS, V_DIM), lambda i: (i, 0, 0)), pl.BlockSpec((tm, att_w), lambda i: (i, 0))],
        out_shape=[jax.ShapeDtypeStruct((m, N_HEADS, V_DIM), F32), jax.ShapeDtypeStruct((m, att_w), BF16)],
        compiler_params=_cparams(("arbitrary",), vmem // MIB + 6),
        name="proj_value",
    )(x_bf, w_bf, w_bf)


def _in_projection(x_f32, w_bf, tm, c_conv, qk_w, att_w, d_model):
    g, x_bf = _proj_glu_cast(x_f32, w_bf, c_conv)
    o = 2 * c_conv
    (sza,) = _proj("za", x_bf, w_bf, (o,), c_conv, _ep_silu, (F32,), tm)
    o += c_conv
    (q_bf,) = _proj("q", x_bf, w_bf, (o,), qk_w, _ep_query, (BF16,), tm)
    o += qk_w
    k, k_bf = _proj("k", x_bf, w_bf, (o,), qk_w, _ep_copy2, (F32, BF16), tm, head_width=HEAD_DIM)
    o += qk_w
    v, v_bf = _proj_value(x_bf, w_bf, o, att_w)
    o += att_w
    (szb,) = _proj("zb", x_bf, w_bf, (o,), att_w, _ep_silu, (F32,), tm)
    o += att_w
    (sgb,) = _proj("gb", x_bf, w_bf, (o + d_model,), d_model, _ep_sigmoid, (F32,), tm)
    return x_bf, g, sza, q_bf, k, k_bf, v, v_bf, szb, sgb, o


def _proj_ga_kernel(x_ref, w_ref, h_ref, wpa_ref, o_ref):
    gate = jax.nn.sigmoid(jnp.dot(x_ref[...], w_ref[...], preferred_element_type=F32))
    o_ref[...] = gate * jnp.dot(h_ref[...], wpa_ref[...], preferred_element_type=F32)


def _proj_ga(x_bf, w_bf, col0, h_bf, wpa_bf, tm):
    m, k = x_bf.shape
    c, d = wpa_bf.shape
    tn = PROJ_TN
    assert col0 % tn == 0 and d % tn == 0
    vmem = 2 * (tm * k * 2 + k * tn * 2 + tm * c * 2 + c * tn * 2 + tm * tn * 4) + 2 * tm * tn * 4
    return pl.pallas_call(
        _proj_ga_kernel,
        grid=(d // tn, m // tm),
        in_specs=[pl.BlockSpec((tm, k), lambda j, i: (i, 0)), pl.BlockSpec((k, tn), lambda j, i: (0, col0 // tn + j)),
                  pl.BlockSpec((tm, c), lambda j, i: (i, 0)), pl.BlockSpec((c, tn), lambda j, i: (0, j))],
        out_specs=pl.BlockSpec((tm, tn), lambda j, i: (i, j)),
        out_shape=jax.ShapeDtypeStruct((m, d), F32),
        compiler_params=_cparams(("arbitrary", "arbitrary"), vmem // MIB + 6),
        name="proj_ga",
    )(x_bf, w_bf, h_bf, wpa_bf)


def _proj_small_kernel(x_ref, w_ref, o_ref, wbf_ref, *, silu_tiles, sigmoid_tiles, query_tiles):
    j = pl.program_id(0)
    in_range = lambda r: (j >= r[0]) & (j < r[1])
    any_range = lambda rs: functools.reduce(jnp.logical_or, [in_range(r) for r in rs])
    w = w_ref[...].astype(BF16)
    wbf_ref[...] = w
    z = jnp.dot(x_ref[...], w, preferred_element_type=F32)
    sg = jax.nn.sigmoid(z)
    o_ref[...] = jnp.where(any_range(silu_tiles), z * sg,
                           jnp.where(any_range(sigmoid_tiles), sg,
                                     jnp.where(any_range(query_tiles), z * (QK_SCALE * LOG2E), z)))


def _in_projection_small(x_bf, w_f32, c_conv, qk_w, att_w, d_model):
    m, k = x_bf.shape
    tn = PROJ_TN
    sizes = (c_conv, c_conv, c_conv, qk_w, qk_w, att_w, att_w, d_model, d_model)
    starts = np.concatenate([[0], np.cumsum(sizes)])
    assert all(s % tn == 0 for s in sizes)
    tiles = lambda i: (int(starts[i]) // tn, int(starts[i + 1]) // tn)
    n_in = int(starts[-1])
    xin, w_bf = pl.pallas_call(
        functools.partial(_proj_small_kernel, silu_tiles=(tiles(2), tiles(6)), sigmoid_tiles=(tiles(7), tiles(8)),
                          query_tiles=(tiles(3),)),
        grid=(n_in // tn,),
        in_specs=[pl.BlockSpec((m, k), lambda j: (0, 0)), pl.BlockSpec((k, tn), lambda j: (0, j))],
        out_specs=[pl.BlockSpec((m, tn), lambda j: (0, j)), pl.BlockSpec((k, tn), lambda j: (0, j))],
        out_shape=[jax.ShapeDtypeStruct((m, n_in), F32), jax.ShapeDtypeStruct((k, n_in), BF16)],
        compiler_params=_cparams(("arbitrary",), (2 * (m * k * 2 + k * tn * 6 + m * tn * 4) + k * tn * 2 + 4 * m * tn * 4) // MIB + 6),
        name="proj_small",
    )(x_bf, w_f32)
    return w_bf, tuple(xin[:, int(starts[i]):int(starts[i + 1])] for i in range(len(sizes)))


def _ln_swish(acc, lng, lnb):
    mu = jnp.mean(acc, axis=-1, keepdims=True)
    xc = acc - mu
    var = jnp.mean(xc * xc, axis=-1, keepdims=True)
    y = xc * lax.rsqrt(var + LN_EPS) * lng + lnb
    return y * jax.nn.sigmoid(y)


def _conv_sample_kernel(state_ref, ua_ref, ug_ref, cw_ref, cb_ref, lng_ref, lnb_ref, sza_ref, sga_ref, wpa_ref,
                        out_ref, g_ref, hs_ref):
    n_state, nb, _ = state_ref.shape
    t_new = ua_ref.shape[0]
    g_ref[...] = ua_ref[...] * jax.nn.sigmoid(ug_ref[...])
    window = lambda r: state_ref[r] if r < n_state else g_ref[r - n_state]
    for t in range(t_new):
        acc = jnp.broadcast_to(cb_ref[...], (nb, cb_ref.shape[-1]))
        for j in range(CONV_WIDTH):
            acc = acc + cw_ref[j:j + 1, :] * window(t + j)
        h = _ln_swish(acc, lng_ref[...], lnb_ref[...]) * sza_ref[t * nb:(t + 1) * nb, :]
        hs_ref[t * nb:(t + 1) * nb, :] = h.astype(BF16)
    out_ref[...] = sga_ref[...] * jnp.dot(hs_ref[...], wpa_ref[...], preferred_element_type=F32)


def _conv_sample(state_tm, ua_tm, ug_tm, conv_w, conv_b, lng, lnb, sza_tm, sga_tm, wpa_bf):
    t_new, nb, c = ua_tm.shape
    d = wpa_bf.shape[1]
    assert state_tm.shape[0] == CONV_WIDTH - 1
    return pl.pallas_call(
        _conv_sample_kernel,
        out_shape=[jax.ShapeDtypeStruct((t_new * nb, d), F32), jax.ShapeDtypeStruct((t_new, nb, c), F32)],
        scratch_shapes=[pltpu.VMEM((t_new * nb, c), BF16)],
        compiler_params=pltpu.CompilerParams(vmem_limit_bytes=32 * MIB),
        name="conv_sample",
    )(state_tm, ua_tm, ug_tm, conv_w, conv_b, lng, lnb, sza_tm, sga_tm, wpa_bf)


def _lambda_value(lq1_ref, lk1_ref, lq2_ref, lk2_ref):
    s1 = jnp.sum(lq1_ref[...] * lk1_ref[...], axis=-1, keepdims=True)
    s2 = jnp.sum(lq2_ref[...] * lk2_ref[...], axis=-1, keepdims=True)
    return jnp.exp(s1) - jnp.exp(s2) + LAM_INIT


def _subln_gate(o, sub, szb):
    r = o * lax.rsqrt(jnp.mean(o * o, axis=-1, keepdims=True) + LN_EPS) * sub * (1.0 - LAM_INIT)
    return (r * szb).astype(BF16)


def _lane_blocks(x):
    return [x[:, i * LANES:(i + 1) * LANES] for i in range(x.shape[1] // LANES)]


def _block_max(s):
    return functools.reduce(jnp.maximum, _lane_blocks(s))


def _softmax_update(s, bmax, m_ref, l_ref, acc_ref, pv_fn):
    m_old = m_ref[...]
    m_new = jnp.maximum(m_old, jnp.max(bmax, axis=-1, keepdims=True))
    alpha = jnp.exp2(m_old - m_new)
    p_blocks = [jnp.exp2(sb - m_new) for sb in _lane_blocks(s)]
    l_ref[...] = alpha * l_ref[...] + functools.reduce(jnp.add, p_blocks)
    pv = pv_fn(jnp.concatenate(p_blocks, axis=1).astype(BF16))
    acc_ref[...] = jnp.concatenate([alpha] * (pv.shape[1] // LANES), axis=1) * acc_ref[...] + pv
    m_ref[...] = m_new


def _bias_by_distance(rel_bias, n_max):
    n = jnp.arange(n_max)
    max_exact = N_BUCKETS // 2
    large = max_exact + (jnp.log(jnp.maximum(n, 1).astype(F32) / max_exact)
                         / math.log(MAX_DISTANCE / max_exact) * (N_BUCKETS - max_exact)).astype(jnp.int32)
    large = jnp.minimum(large, N_BUCKETS - 1)
    bucket = jnp.where(n < max_exact, n, large)
    return jnp.transpose(rel_bias[bucket].astype(F32), (1, 0))


def _toeplitz_blocks(bdl):
    assert MAX_DISTANCE <= LANES
    per = 2 * LANES
    n = np.arange(per)
    blocks = []
    for e in range(2):
        idx = np.where(n < LANES, np.maximum(LANES * e - n, 0), LANES * e + per - n)
        w = bdl[:, idx]
        x = jnp.tile(w, (1, LANES))[:, :LANES * (per - 1)].reshape(-1, LANES, per - 1)
        blocks.append(x[:, :, :LANES])
    return jnp.stack(blocks, axis=1)


def _attn_prompt_kernel(cfar_ref, q_ref, k_ref, v_ref, u_ref, szb_ref, sub_ref, lq1_ref, lk1_ref, lq2_ref, lk2_ref,
                        out_ref, strip_ref, s_ref, bm_ref, m_ref, l_ref, acc_ref, *, tq, tk):
    h = pl.program_id(1)
    qi = pl.program_id(2)
    far = tq + 2 * LANES

    @pl.when(qi == 0)
    def _():
        strip_ref[...] = jnp.full(strip_ref.shape, cfar_ref[h], F32)
        strip_ref[tq:tq + LANES, :] = u_ref[0, 0]
        strip_ref[tq + LANES:far, :] = u_ref[0, 1]

    q = q_ref[0]
    qs = (q[:, :HEAD_DIM], q[:, HEAD_DIM:])
    m_ref[...] = jnp.full(m_ref.shape, NEG_INF, F32)
    l_ref[...] = jnp.zeros(l_ref.shape, F32)
    acc_ref[...] = jnp.zeros(acc_ref.shape, F32)

    def scores(j, slot):
        off = pl.multiple_of(j * tk, tk)
        kj = k_ref[0, pl.ds(off, tk), :]
        r0 = tq + (qi - j) * tk
        bias = [strip_ref[pl.ds(pl.multiple_of(jnp.minimum(r0 - cb * LANES, far), LANES), tq), :]
                for cb in range(tk // LANES)]
        for c in range(2):
            qk = lax.dot_general(qs[c], kj[:, c * HEAD_DIM:(c + 1) * HEAD_DIM], (((1,), (1,)), ((), ())),
                                 preferred_element_type=F32)
            s_blocks = [sb + bb for sb, bb in zip(_lane_blocks(qk), bias)]
            s_ref[slot, c] = jnp.concatenate(s_blocks, axis=1)
            bm_ref[slot, c] = functools.reduce(jnp.maximum, s_blocks)

    def consume(j, slot, mask):
        off = pl.multiple_of(j * tk, tk)
        vj = v_ref[0, pl.ds(off, tk), :]
        for c in range(2):
            s = s_ref[slot, c]
            if mask is None:
                bmax = bm_ref[slot, c]
            else:
                s = jnp.where(mask, s, NEG_INF)
                bmax = _block_max(s)
            _softmax_update(s, bmax, m_ref.at[c], l_ref.at[c], acc_ref.at[c],
                            lambda p: jnp.dot(p, vj, preferred_element_type=F32))

    scores(0, 0)

    def body(j, carry):
        consume(j, j % 2, None)
        scores(j + 1, (j + 1) % 2)
        return carry

    lax.fori_loop(0, qi, body, 0)

    row = lax.broadcasted_iota(jnp.int32, (tq, tk), 0)
    col = lax.broadcasted_iota(jnp.int32, (tq, tk), 1)
    consume(qi, qi % 2, col <= row)

    lam = _lambda_value(lq1_ref, lk1_ref, lq2_ref, lk2_ref)
    o1 = acc_ref[0] / jnp.sum(l_ref[0], axis=-1, keepdims=True)
    o2 = acc_ref[1] / jnp.sum(l_ref[1], axis=-1, keepdims=True)
    out_ref[0] = _subln_gate(o1 - lam * o2, sub_ref[...], szb_ref[0])


def _attn_prompt(q_bf, k_bf, v_bf, szb, bdl, sub, lq1, lk1, lq2, lk2):
    b, t, _ = q_bf.shape
    tq, tk = ATT_TQ, ATT_TK
    assert tq == tk and tq % LANES == 0 and t % tq == 0
    u = _toeplitz_blocks(bdl)
    cfar = bdl[:, MAX_DISTANCE]
    smem = pl.BlockSpec(memory_space=pltpu.SMEM)
    vec = pl.BlockSpec((1, HEAD_DIM), lambda bi, h, i: (0, 0))
    strip_rows = 2 * tq + 2 * LANES
    vmem = (2 * (2 * t * V_DIM * 2 + 2 * LANES * LANES * 4 + tq * V_DIM * (2 + 4 + 2))
            + 2 * 2 * tq * tk * 4 + (strip_rows + (2 * 2 + 2 * 2) * tq) * LANES * 4 + 2 * tq * V_DIM * 4
            + 4 * tq * tk * 4)
    return pl.pallas_call(
        functools.partial(_attn_prompt_kernel, tq=tq, tk=tk),
        grid=(b, N_HEADS, t // tq),
        in_specs=[
            smem,
            pl.BlockSpec((1, tq, V_DIM), lambda bi, h, i: (bi, i, h)),
            pl.BlockSpec((1, t, V_DIM), lambda bi, h, i: (bi, 0, h)),
            pl.BlockSpec((1, t, V_DIM), lambda bi, h, i: (bi, 0, h)),
            pl.BlockSpec((1, 2, LANES, LANES), lambda bi, h, i: (h, 0, 0, 0)),
            pl.BlockSpec((1, tq, V_DIM), lambda bi, h, i: (bi, i, h)),
            pl.BlockSpec((1, V_DIM), lambda bi, h, i: (0, 0)),
            vec, vec, vec, vec,
        ],
        out_specs=pl.BlockSpec((1, tq, V_DIM), lambda bi, h, i: (bi, i, h)),
        out_shape=jax.ShapeDtypeStruct((b, t, N_HEADS * V_DIM), BF16),
        scratch_shapes=[pltpu.VMEM((strip_rows, LANES), F32), pltpu.VMEM((2, 2, tq, tk), F32),
                        pltpu.VMEM((2, 2, tq, LANES), F32), pltpu.VMEM((2, tq, LANES), F32),
                        pltpu.VMEM((2, tq, LANES), F32), pltpu.VMEM((2, tq, V_DIM), F32)],
        compiler_params=_cparams(("arbitrary", "arbitrary", "arbitrary"), vmem // MIB + 6),
        name="attn_prompt",
    )(cfar, q_bf, k_bf, v_bf, u, szb, sub, lq1, lk1, lq2, lk2)


def _conv_rows(hist, cur, xs_ref, cw_ref, cb_ref, lng_ref, lnb_ref, sza):
    rb = cur.shape[0]
    n = 2 * rb
    xs_ref[0, 0:rb, :] = hist
    xs_ref[0, rb:n, :] = cur
    for r in range(1, V7X_SUBLANES):
        xs_ref[r, 0:n - V7X_SUBLANES, :] = xs_ref[0, r:r + n - V7X_SUBLANES, :]
    first = rb - (CONV_WIDTH - 1)
    acc = jnp.broadcast_to(cb_ref[...], cur.shape)
    for j in range(CONV_WIDTH):
        r, a = (first + j) % V7X_SUBLANES, (first + j) // V7X_SUBLANES
        acc = acc + cw_ref[j:j + 1, :] * xs_ref[r, V7X_SUBLANES * a:V7X_SUBLANES * a + rb, :]
    return (_ln_swish(acc, lng_ref[...], lnb_ref[...]) * sza).astype(BF16)


def _attn_decode_kernel(pt_ref, qc_ref, bfar_ref, bnear_ref, knew_ref, vnew_ref, bnew_ref, szb_ref, sub_ref,
                        lq1_ref, lk1_ref, lq2_ref, lk2_ref, gcur_ref, gprev_ref, sza_ref, cw_ref, cb_ref, lng_ref, lnb_ref,
                        *refs, n_pages, t_new, conv_blocks_per_seq):
    del pt_ref
    k_refs, v_refs = refs[:n_pages], refs[n_pages:2 * n_pages]
    out_ref, h_ref, m_ref, l_ref, acc_ref, xs_ref = refs[2 * n_pages:]
    s_idx = pl.program_id(1)

    last = s_idx == pl.num_programs(1) - 1
    nrow = qc_ref.shape[2]
    cols = k_refs[0].shape[1] // 2
    n_near = bnear_ref.shape[0]
    nt = (((1,), (1,)), ((), ()))
    qcs = (qc_ref[0, 0], qc_ref[0, 1])

    @pl.when(s_idx == 0)
    def _():
        m_ref[...] = jnp.full(m_ref.shape, NEG_INF, F32)
        l_ref[...] = jnp.zeros(l_ref.shape, F32)
        acc_ref[...] = jnp.zeros(acc_ref.shape, F32)

    bfar = bfar_ref[...]
    bias = [bfar] * (n_pages - n_near) + [jnp.where(last, bnear_ref[i], bfar) for i in range(n_near)]
    group = max(n_pages // DEC_GROUPS, 1)
    for lo in range(0, n_pages, group):
        pages = range(lo, min(lo + group, n_pages))
        s_maps = []
        for c in range(2):
            parts = [lax.dot_general(qcs[c], k_refs[i][0, pl.ds(c, cols, stride=2), :].astype(BF16), nt,
                                     preferred_element_type=F32) + bias[i] for i in pages]
            s_maps.append(jnp.concatenate(parts, axis=1))
        s_grp = jnp.concatenate(s_maps, axis=0)

        def pv_grp(p, pages=pages):
            out = None
            for n, i in enumerate(pages):
                term = jnp.dot(p[:, n * cols:(n + 1) * cols], v_refs[i][0].astype(BF16), preferred_element_type=F32)
                out = term if out is None else out + term
            return out

        _softmax_update(s_grp, _block_max(s_grp), m_ref, l_ref, acc_ref, pv_grp)

    blk = pl.program_id(0) * pl.num_programs(1) + s_idx
    hist = jnp.where(blk % conv_blocks_per_seq == 0, 0.0, gprev_ref[0])
    h_ref[0] = _conv_rows(hist, gcur_ref[0], xs_ref, cw_ref, cb_ref, lng_ref, lnb_ref, sza_ref[0])

    @pl.when(last)
    def _():
        s_new = jnp.concatenate([lax.dot_general(qcs[c], knew_ref[0, c], nt, preferred_element_type=F32) + bnew_ref[...]
                                 for c in range(2)], axis=0)
        row = lax.broadcasted_iota(jnp.int32, s_new.shape, 0)
        col = lax.broadcasted_iota(jnp.int32, s_new.shape, 1)
        valid = (col % N_HEADS == (row % nrow) // t_new) & (col // N_HEADS <= row % t_new)
        s_new = jnp.where(valid, s_new, NEG_INF)
        _softmax_update(s_new, _block_max(s_new), m_ref, l_ref, acc_ref,
                        lambda p: jnp.dot(p, vnew_ref[0], preferred_element_type=F32))

        lam = _lambda_value(lq1_ref, lk1_ref, lq2_ref, lk2_ref)
        o = acc_ref[...] / jnp.sum(l_ref[...], axis=-1, keepdims=True)
        out_ref[0] = _subln_gate(o[0:nrow] - lam * o[nrow:2 * nrow], sub_ref[...], szb_ref[0])


def _decode_bias(bdl, past, page, tn):
    nrow = N_HEADS * tn
    same_head = jnp.eye(N_HEADS, dtype=bool)[:, None, None, :]
    n_far = (past - MAX_DISTANCE + 1) // page
    n_near = past // page - n_far
    far = jnp.broadcast_to(jnp.where(same_head, bdl[:, MAX_DISTANCE][:, None, None, None], NEG_INF),
                           (N_HEADS, tn, page, N_HEADS)).reshape(nrow, page * N_HEADS)
    span = n_near * page
    near = jnp.stack([jnp.flip(bdl[:, qi + 1:qi + 1 + span], axis=1) for qi in range(tn)], axis=1)
    near = jnp.where(same_head, near[..., None], NEG_INF).reshape(nrow, n_near, page * N_HEADS)
    new_tok = LANES // N_HEADS
    dist_new = np.maximum(np.arange(tn)[:, None] - np.arange(new_tok)[None, :], 0)
    new = jnp.where(same_head, bdl[:, dist_new][..., None], 0.0).reshape(nrow, LANES)
    return far, jnp.transpose(near, (1, 0, 2)), new


def _attn_decode(page_table, cache_kr, cache_vr, qc, bias_far, bias_near, k_new, v_new, bias_new, szb, sub,
                 lq1, lk1, lq2, lk2, g, sza, conv_w, conv_b, clng, clnb):
    nb, n_pt = page_table.shape
    _, k_rows, _ = cache_kr.shape
    _, v_rows, _ = cache_vr.shape
    nrow = qc.shape[2]
    t_new = nrow // N_HEADS
    npg = DEC_PAGES
    assert n_pt % npg == 0 and k_rows == 2 * v_rows and bias_near.shape[0] <= npg
    new_rows = k_new.shape[2]

    def page_spec(rows, width, i):
        return pl.BlockSpec((1, rows, width), lambda b, s, pt, i=i: (pt[b, s * npg + i], 0, 0))

    ns = n_pt // npg
    bp, t, c = g.shape
    rb = bp * t // (nb * ns)
    assert rb * nb * ns == bp * t and rb % V7X_SUBLANES == 0 and rb >= CONV_WIDTH - 1 and t % rb == 0
    cps = t // rb
    cblk = lambda b, s, pt: ((b * ns + s) // cps, (b * ns + s) % cps, 0)
    cprev = lambda b, s, pt: ((b * ns + s) // cps, jnp.maximum((b * ns + s) % cps - 1, 0), 0)
    cconst = lambda shape: pl.BlockSpec(shape, lambda b, s, pt: (0,) * len(shape))
    vec = pl.BlockSpec((1, HEAD_DIM), lambda b, s, pt: (0, 0))
    in_specs = [
        pl.BlockSpec((1, 2, nrow, HEAD_DIM), lambda b, s, pt: (b, 0, 0, 0)),
        pl.BlockSpec(bias_far.shape, lambda b, s, pt: (0, 0)),
        pl.BlockSpec(bias_near.shape, lambda b, s, pt: (0, 0, 0)),
        pl.BlockSpec((1, 2, new_rows, HEAD_DIM), lambda b, s, pt: (b, 0, 0, 0)),
        pl.BlockSpec((1, new_rows, V_DIM), lambda b, s, pt: (b, 0, 0)),
        pl.BlockSpec((nrow, new_rows), lambda b, s, pt: (0, 0)),
        pl.BlockSpec((1, nrow, V_DIM), lambda b, s, pt: (b, 0, 0)),
        pl.BlockSpec((1, V_DIM), lambda b, s, pt: (0, 0)),
        vec, vec, vec, vec,
        pl.BlockSpec((1, rb, c), cblk), pl.BlockSpec((1, rb, c), cprev), pl.BlockSpec((1, rb, c), cblk),
        cconst((CONV_WIDTH, c)), cconst((1, c)), cconst((1, c)), cconst((1, c)),
    ] + [page_spec(k_rows, HEAD_DIM, i) for i in range(npg)] + [page_spec(v_rows, V_DIM, i) for i in range(npg)]
    page_bytes = k_rows * HEAD_DIM * 4
    vmem = 2 * 2 * npg * page_bytes + 10 * 2 * nrow * npg * v_rows * 4 + 24 * rb * c * 4
    return pl.pallas_call(
        functools.partial(_attn_decode_kernel, n_pages=npg, t_new=t_new, conv_blocks_per_seq=cps),
        grid_spec=pltpu.PrefetchScalarGridSpec(
            num_scalar_prefetch=1,
            grid=(nb, ns),
            in_specs=in_specs,
            out_specs=[pl.BlockSpec((1, nrow, V_DIM), lambda b, s, pt: (b, 0, 0)), pl.BlockSpec((1, rb, c), cblk)],
            scratch_shapes=[pltpu.VMEM((2 * nrow, LANES), F32), pltpu.VMEM((2 * nrow, LANES), F32),
                            pltpu.VMEM((2 * nrow, V_DIM), F32), pltpu.VMEM((V7X_SUBLANES, 2 * rb, c), F32)],
        ),
        out_shape=[jax.ShapeDtypeStruct((nb, nrow, V_DIM), BF16), jax.ShapeDtypeStruct((bp, t, c), BF16)],
        compiler_params=_cparams(("arbitrary", "arbitrary"), vmem // MIB + 6),
        name="attn_decode",
    )(page_table, qc, bias_far, bias_near, k_new, v_new, bias_new, szb, sub, lq1, lk1, lq2, lk2,
      g, g, sza, conv_w, conv_b, clng, clnb,
      *([cache_kr] * npg), *([cache_vr] * npg))


def _final_kernel(hb_ref, ma_ref, sgb_ref, x_ref, p_ref, wpb_ref, wout_ref, wgate_ref, wple_ref, lng_ref, lnb_ref, y_ref):
    b_proj = jnp.dot(hb_ref[...], wpb_ref[...], preferred_element_type=F32)
    merged = ma_ref[...] + sgb_ref[...] * b_proj
    z = ALPHA * x_ref[...] + jnp.dot(merged.astype(BF16), wout_ref[...], preferred_element_type=F32)
    mu = jnp.mean(z, axis=-1, keepdims=True)
    zc = z - mu
    var = jnp.mean(zc * zc, axis=-1, keepdims=True)
    h = zc * lax.rsqrt(var + LN_EPS) * lng_ref[...] + lnb_ref[...]
    gate = jax.nn.sigmoid(jnp.dot(h.astype(BF16), wgate_ref[...], preferred_element_type=F32))
    pe = jnp.dot(p_ref[...].astype(BF16), wple_ref[...], preferred_element_type=F32)
    y_ref[...] = h + gate * pe


def _final(hb, ma, sgb, sgb_block, x, p, wpb_bf, wout_bf, wgate_bf, wple_bf, lng, lnb, tm):
    m, d = x.shape
    pd = p.shape[1]
    row = lambda w: pl.BlockSpec((tm, w), lambda i: (i, 0))
    const = lambda r, c: pl.BlockSpec((r, c), lambda i: (0, 0), pipeline_mode=pl.Buffered(1))
    vmem = 2 * tm * (d * 2 + 4 * d * 4 + pd * 4) + (3 * d * d + pd * d) * 2 + 6 * tm * d * 4
    return pl.pallas_call(
        _final_kernel,
        grid=(m // tm,),
        in_specs=[row(d), row(d), pl.BlockSpec((tm, d), lambda i: (i, sgb_block)), row(d), row(pd),
                  const(d, d), const(d, d), const(d, d), const(pd, d), const(1, d), const(1, d)],
        out_specs=row(d),
        out_shape=jax.ShapeDtypeStruct((m, d), F32),
        compiler_params=_cparams(("arbitrary",), vmem // MIB + 6),
        name=f"final_m{m}",
    )(hb, ma, sgb, x, p, wpb_bf, wout_bf, wgate_bf, wple_bf, lng, lnb)


def kernel(x_prompt, x_sample, p_prompt, p_sample, cache_k, cache_v, state_conv, page_table, w_in, conv_w, conv_b, conv_ln_g, conv_ln_b, w_proj_a, lambda_q1, lambda_k1, lambda_q2, lambda_k2, subln_w, w_proj_b, w_out, ln_g, ln_b, w_ple_proj, w_ple_gate, rel_bias):
    assert w_in.shape[0] == DEPTH
    b, t, d = x_prompt.shape
    nb, tn, _ = x_sample.shape
    c = conv_w.shape[-1]
    n_maps = 2 * N_HEADS
    qk_w = n_maps * HEAD_DIM
    att_w = N_HEADS * V_DIM
    pd = p_prompt.shape[-1]
    n_phys, page = cache_k.shape[1], cache_k.shape[2]
    past = page_table.shape[1] * page

    wpa_bf = w_proj_a[0].astype(BF16)
    wpb_bf = w_proj_b[0].astype(BF16)
    wout_bf = w_out[0].astype(BF16)
    wgate_bf = w_ple_gate[0].astype(BF16)
    wple_bf = w_ple_proj[0].astype(BF16)
    cw, cb = conv_w[0], conv_b[0].reshape(1, c)
    clng, clnb = conv_ln_g[0].reshape(1, c), conv_ln_b[0].reshape(1, c)
    lng, lnb = ln_g[0].reshape(1, d), ln_b[0].reshape(1, d)
    sub = subln_w[0].reshape(1, V_DIM)
    lq1, lk1, lq2, lk2 = (v[0].reshape(1, HEAD_DIM) for v in (lambda_q1, lambda_k1, lambda_q2, lambda_k2))
    bdl = _bias_by_distance(rel_bias, MAX_DISTANCE + 2 * LANES + page + tn) * LOG2E

    ms = nb * tn
    xs = x_sample.reshape(ms, d)
    w_bf, (ua_s, ug_s, sza_s, q_s, k_s, v_s, szb_s, sga_s, sgb_s) = _in_projection_small(
        xs.astype(BF16), w_in[0], c, qk_w, att_w, d)

    xp = x_prompt.reshape(b * t, d)
    xp_bf, g, sza, q_bf, k_p, k_bf, v_p, v_bf, szb, sgb, ga_col0 = _in_projection(xp, w_bf, PROJ_TM, c, qk_w, att_w, d)
    g3 = g.reshape(b, t, c)
    hb = _attn_prompt(q_bf.reshape(b, t, qk_w), k_bf.reshape(b, t, qk_w), v_bf.reshape(b, t, att_w),
                      szb.reshape(b, t, att_w), bdl, sub, lq1, lk1, lq2, lk2)

    q_s, k_s_bf, v_s_bf = q_s.astype(BF16), k_s.astype(BF16), v_s.astype(BF16)
    tm_major = lambda a: jnp.transpose(a.reshape(nb, tn, a.shape[-1]), (1, 0, 2))
    ma_s_tm, g_s_tm = _conv_sample(jnp.transpose(state_conv[0], (1, 0, 2)), tm_major(ua_s), tm_major(ug_s), cw, cb,
                                   clng, clnb, tm_major(sza_s).reshape(ms, c), tm_major(sga_s).reshape(ms, d), wpa_bf)
    ma_s = jnp.transpose(ma_s_tm.reshape(tn, nb, d), (1, 0, 2)).reshape(ms, d)
    g_s = jnp.transpose(g_s_tm, (1, 0, 2))

    nrow = N_HEADS * tn
    qc = jnp.transpose(q_s.reshape(nb, tn, N_HEADS, 2, HEAD_DIM), (0, 3, 2, 1, 4)).reshape(nb, 2, nrow, HEAD_DIM)
    k_new = jnp.transpose(k_s_bf.reshape(nb, tn, N_HEADS, 2, HEAD_DIM), (0, 3, 1, 2, 4)).reshape(nb, 2, nrow, HEAD_DIM)
    k_new = jnp.pad(k_new, ((0, 0), (0, 0), (0, LANES - nrow), (0, 0)))
    v_new = jnp.pad(v_s_bf.reshape(nb, nrow, V_DIM), ((0, 0), (0, LANES - nrow), (0, 0)))
    szb_hq = jnp.transpose(szb_s.reshape(nb, tn, N_HEADS, V_DIM), (0, 2, 1, 3)).reshape(nb, nrow, V_DIM)
    bias_far, bias_near, bias_new = _decode_bias(bdl, past, page, tn)
    hq, h_conv = _attn_decode(page_table, cache_k.reshape(n_phys, page * n_maps, HEAD_DIM),
                              cache_v.reshape(n_phys, page * N_HEADS, V_DIM), qc, bias_far, bias_near, k_new, v_new,
                              bias_new, szb_hq, sub, lq1, lk1, lq2, lk2, g3, sza.reshape(b, t, c), cw, cb, clng, clnb)
    hb_s = jnp.transpose(hq.reshape(nb, N_HEADS, tn, V_DIM), (0, 2, 1, 3)).reshape(ms, att_w)
    y_s = _final(hb_s, ma_s, sgb_s, 0, xs, p_sample[0].reshape(ms, pd),
                 wpb_bf, wout_bf, wgate_bf, wple_bf, lng, lnb, ms)

    ma = _proj_ga(xp_bf, w_bf, ga_col0, h_conv.reshape(b * t, c), wpa_bf, PROJ_TM)
    y_p = _final(hb.reshape(b * t, att_w), ma, sgb, 0, xp, p_prompt[0].reshape(b * t, pd),
                 wpb_bf, wout_bf, wgate_bf, wple_bf, lng, lnb, FINAL_TM)

    conv_prompt = g3[:, t - (CONV_WIDTH - 1):, :]
    conv_sample = jnp.concatenate([state_conv[0][:, tn:, :], g_s], axis=1)
    return (y_p.reshape(b, t, d), y_s.reshape(nb, tn, d),
            k_p.reshape(1, b, t, n_maps, HEAD_DIM), v_p.reshape(1, b, t, N_HEADS, V_DIM), conv_prompt[None],
            k_s.reshape(1, nb, tn, n_maps, HEAD_DIM), v_s.reshape(1, nb, tn, N_HEADS, V_DIM), conv_sample[None])
```

```python
import functools
import math

import numpy as np
import jax
import jax.numpy as jnp
from jax import lax
from jax.experimental import pallas as pl
from jax.experimental.pallas import tpu as pltpu

F32 = jnp.float32
BF16 = jnp.bfloat16

N_HEADS = 8
HEAD_DIM = 128
V_DIM = 2 * HEAD_DIM
CONV_WIDTH = 31
N_BUCKETS = 32
MAX_DISTANCE = 128
LN_EPS = 1e-5
NEG_INF = -1e30
DEPTH = 1
ALPHA = (2 * DEPTH) ** 0.25
LAM_INIT = 0.8 - 0.6 * math.exp(-0.3 * 0)
QK_SCALE = HEAD_DIM ** -0.5
LOG2E = math.log2(math.e)

V7X_VMEM_BYTES = 64 * 1024 * 1024
V7X_SUBLANES = 8
LANES = 128
MIB = 1024 * 1024

PROJ_TN = 1024
PROJ_TM = 1024
GLU_TM = 512
ATT_TQ = 512
ATT_TK = 512
DEC_PAGES = 8
DEC_GROUPS = 4
FINAL_TM = 256


def _cparams(sem, vmem_mib):
    limit = min(vmem_mib * MIB, V7X_VMEM_BYTES - 6 * MIB)
    return pltpu.CompilerParams(dimension_semantics=sem, vmem_limit_bytes=limit)


def _proj_kernel(x_ref, *refs, n_w, epilogue):
    w_refs, out_refs = refs[:n_w], refs[n_w:]
    x = x_ref[...]
    accs = [jnp.dot(x, w[...], preferred_element_type=F32) for w in w_refs]
    for o_ref, o in zip(out_refs, epilogue(*accs)):
        o_ref[...] = o.astype(o_ref.dtype).reshape(o_ref.shape)


def _proj(name, x_bf, w_bf, col_starts, ncols, epilogue, out_dtypes, tm, head_width=None):
    m, k = x_bf.shape
    tn = PROJ_TN
    n_w = len(col_starts)
    in_specs = [pl.BlockSpec((tm, k), lambda j, i: (i, 0))]
    for c0 in col_starts:
        assert c0 % tn == 0
        in_specs.append(pl.BlockSpec((k, tn), lambda j, i, c0=c0: (0, c0 // tn + j)))
    out_specs = [pl.BlockSpec((tm, tn), lambda j, i: (i, j)) for _ in out_dtypes]
    out_shape = [jax.ShapeDtypeStruct((m, ncols), dt) for dt in out_dtypes]
    if head_width is not None:
        assert (tn // head_width) % V7X_SUBLANES == 0
        out_specs[0] = pl.BlockSpec((tm, tn // head_width, head_width), lambda j, i: (i, j, 0))
        out_shape[0] = jax.ShapeDtypeStruct((m, ncols // head_width, head_width), out_dtypes[0])
    vmem = 2 * (tm * k * 2 + n_w * k * tn * 2 + sum(tm * tn * jnp.dtype(d).itemsize for d in out_dtypes))
    vmem += n_w * tm * tn * 4
    return pl.pallas_call(
        functools.partial(_proj_kernel, n_w=n_w, epilogue=epilogue),
        grid=(ncols // tn, m // tm),
        in_specs=in_specs, out_specs=out_specs, out_shape=out_shape,
        compiler_params=_cparams(("arbitrary", "arbitrary"), vmem // MIB + 6),
        name=f"proj_{name}_m{m}",
    )(x_bf, *([w_bf] * n_w))


def _ep_silu(z):
    return (jax.nn.silu(z),)


def _ep_sigmoid(z):
    return (jax.nn.sigmoid(z),)


def _ep_query(z):
    return (z * (QK_SCALE * LOG2E),)


def _ep_copy2(z):
    return (z, z)


def _glu_cast_kernel(x_ref, wa_ref, wg_ref, wz_ref, g_ref, sza_ref, xbf_ref):
    xb = x_ref[...].astype(BF16)
    xbf_ref[...] = xb
    ua = jnp.dot(xb, wa_ref[...], preferred_element_type=F32)
    ug = jnp.dot(xb, wg_ref[...], preferred_element_type=F32)
    g_ref[...] = ua * jax.nn.sigmoid(ug)
    sza_ref[...] = jax.nn.silu(jnp.dot(xb, wz_ref[...], preferred_element_type=F32))


def _proj_glu_cast(x_f32, w_bf, c_conv):
    m, k = x_f32.shape
    tm, tn = GLU_TM, PROJ_TN
    assert c_conv == tn and m % tm == 0
    vmem = 2 * (tm * k * 4 + 3 * k * tn * 2 + 2 * tm * tn * 4 + tm * k * 2) + 3 * tm * tn * 4 + tm * k * 2
    wspec = lambda n: pl.BlockSpec((k, tn), lambda i: (0, n))
    return pl.pallas_call(
        _glu_cast_kernel,
        grid=(m // tm,),
        in_specs=[pl.BlockSpec((tm, k), lambda i: (i, 0)), wspec(0), wspec(1), wspec(2)],
        out_specs=[pl.BlockSpec((tm, tn), lambda i: (i, 0)), pl.BlockSpec((tm, tn), lambda i: (i, 0)),
                   pl.BlockSpec((tm, k), lambda i: (i, 0))],
        out_shape=[jax.ShapeDtypeStruct((m, c_conv), F32), jax.ShapeDtypeStruct((m, c_conv), F32),
                   jax.ShapeDtypeStruct((m, k), BF16)],
        compiler_params=_cparams(("arbitrary",), vmem // MIB + 6),
        name="proj_glu_cast",
    )(x_f32, w_bf, w_bf, w_bf)


def _proj_value_kernel(x_ref, wa_ref, wb_ref, o_ref, obf_ref):
    x = x_ref[...]
    z = jnp.concatenate([jnp.dot(x, wa_ref[...], preferred_element_type=F32),
                         jnp.dot(x, wb_ref[...], preferred_element_type=F32)], axis=1)
    o_ref[...] = z.reshape(o_ref.shape)
    obf_ref[...] = z.astype(BF16)


def _proj_value(x_bf, w_bf, col0, att_w):
    m, k = x_bf.shape
    tm, tn = GLU_TM, PROJ_TN
    assert att_w == 2 * tn == N_HEADS * V_DIM and col0 % tn == 0 and m % tm == 0
    vmem = 2 * (tm * k * 2 + 2 * k * tn * 2 + tm * att_w * 6) + 2 * tm * att_w * 4
    return pl.pallas_call(
        _proj_value_kernel,
        grid=(m // tm,),
        in_specs=[pl.BlockSpec((tm, k), lambda i: (i, 0)), pl.BlockSpec((k, tn), lambda i: (0, col0 // tn)),
                  pl.BlockSpec((k, tn), lambda i: (0, col0 // tn + 1))],
        out_specs=[pl.BlockSpec((tm, N_HEADS, V_DIM), lambda i: (i, 0, 0)), pl.BlockSpec((tm, att_w), lambda i: (i, 0))],
        out_shape=[jax.ShapeDtypeStruct((m, N_HEADS, V_DIM), F32), jax.ShapeDtypeStruct((m, att_w), BF16)],
        compiler_params=_cparams(("arbitrary",), vmem // MIB + 6),
        name="proj_value",
    )(x_bf, w_bf, w_bf)


def _in_projection(x_f32, w_bf, tm, c_conv, qk_w, att_w, d_model):
    g, sza, x_bf = _proj_glu_cast(x_f32, w_bf, c_conv)
    o = 3 * c_conv
    (q_bf,) = _proj("q", x_bf, w_bf, (o,), qk_w, _ep_query, (BF16,), tm)
    o += qk_w
    k, k_bf = _proj("k", x_bf, w_bf, (o,), qk_w, _ep_copy2, (F32, BF16), tm, head_width=HEAD_DIM)
    o += qk_w
    v, v_bf = _proj_value(x_bf, w_bf, o, att_w)
    o += att_w
    (szb,) = _proj("zb", x_bf, w_bf, (o,), att_w, _ep_silu, (F32,), tm)
    o += att_w
    (sgb,) = _proj("gb", x_bf, w_bf, (o + d_model,), d_model, _ep_sigmoid, (F32,), tm)
    return x_bf, g, sza, q_bf, k, k_bf, v, v_bf, szb, sgb, o


def _proj_ga_kernel(x_ref, w_ref, h_ref, wpa_ref, o_ref):
    gate = jax.nn.sigmoid(jnp.dot(x_ref[...], w_ref[...], preferred_element_type=F32))
    o_ref[...] = gate * jnp.dot(h_ref[...], wpa_ref[...], preferred_element_type=F32)


def _proj_ga(x_bf, w_bf, col0, h_bf, wpa_bf, tm):
    m, k = x_bf.shape
    c, d = wpa_bf.shape
    tn = PROJ_TN
    assert col0 % tn == 0 and d % tn == 0
    vmem = 2 * (tm * k * 2 + k * tn * 2 + tm * c * 2 + c * tn * 2 + tm * tn * 4) + 2 * tm * tn * 4
    return pl.pallas_call(
        _proj_ga_kernel,
        grid=(d // tn, m // tm),
        in_specs=[pl.BlockSpec((tm, k), lambda j, i: (i, 0)), pl.BlockSpec((k, tn), lambda j, i: (0, col0 // tn + j)),
                  pl.BlockSpec((tm, c), lambda j, i: (i, 0)), pl.BlockSpec((c, tn), lambda j, i: (0, j))],
        out_specs=pl.BlockSpec((tm, tn), lambda j, i: (i, j)),
        out_shape=jax.ShapeDtypeStruct((m, d), F32),
        compiler_params=_cparams(("arbitrary", "arbitrary"), vmem // MIB + 6),
        name="proj_ga",
    )(x_bf, w_bf, h_bf, wpa_bf)


def _proj_small_kernel(x_ref, w_ref, *refs, groups):
    j = pl.program_id(0)
    wbf_ref, out_refs = refs[-1], refs[:-1]
    w = w_ref[...].astype(BF16)
    wbf_ref[...] = w
    z = jnp.dot(x_ref[...], w, preferred_element_type=F32)
    for o_ref, (lo, hi, fn) in zip(out_refs, groups):
        @pl.when((j >= lo) & (j < hi))
        def _(o_ref=o_ref, fn=fn):
            o_ref[...] = fn(z).astype(o_ref.dtype).reshape(o_ref.shape)


def _in_projection_small(x_bf, w_f32, c_conv, qk_w, att_w, d_model):
    m, k = x_bf.shape
    tn = PROJ_TN
    sizes = (c_conv, c_conv, c_conv, qk_w, qk_w, att_w, att_w, d_model, d_model)
    starts = np.concatenate([[0], np.cumsum(sizes)])
    assert all(s % tn == 0 for s in sizes)
    n_in = int(starts[-1])
    ident = lambda z: z
    query = lambda z: z * (QK_SCALE * LOG2E)
    outs = [(0, ident, F32, None), (1, ident, F32, None), (2, jax.nn.silu, F32, None), (3, query, BF16, None),
            (4, ident, F32, HEAD_DIM), (4, ident, BF16, None), (5, ident, F32, None), (5, ident, BF16, None),
            (6, jax.nn.silu, F32, None), (7, jax.nn.sigmoid, F32, None), (8, jax.nn.sigmoid, F32, None)]
    groups, out_specs, out_shape = [], [], []
    for grp, fn, dt, hw in outs:
        lo, hi = int(starts[grp]) // tn, int(starts[grp + 1]) // tn
        groups.append((lo, hi, fn))
        pin = lambda j, lo=lo, hi=hi: jnp.clip(j - lo, 0, hi - lo - 1)
        if hw is None:
            out_specs.append(pl.BlockSpec((m, tn), lambda j, pin=pin: (0, pin(j))))
            out_shape.append(jax.ShapeDtypeStruct((m, sizes[grp]), dt))
        else:
            assert (tn // hw) % V7X_SUBLANES == 0
            out_specs.append(pl.BlockSpec((m, tn // hw, hw), lambda j, pin=pin: (0, pin(j), 0)))
            out_shape.append(jax.ShapeDtypeStruct((m, sizes[grp] // hw, hw), dt))
    out_specs.append(pl.BlockSpec((k, tn), lambda j: (0, j)))
    out_shape.append(jax.ShapeDtypeStruct((k, n_in), BF16))
    vmem = 2 * (m * k * 2 + k * tn * 6 + len(outs) * m * tn * 4) + k * tn * 2 + 4 * m * tn * 4
    res = pl.pallas_call(
        functools.partial(_proj_small_kernel, groups=tuple(groups)),
        grid=(n_in // tn,),
        in_specs=[pl.BlockSpec((m, k), lambda j: (0, 0)), pl.BlockSpec((k, tn), lambda j: (0, j))],
        out_specs=out_specs, out_shape=out_shape,
        compiler_params=_cparams(("arbitrary",), vmem // MIB + 6),
        name="proj_small",
    )(x_bf, w_f32)
    return res[-1], tuple(res[:-1])


def _ln_swish(acc, lng, lnb):
    mu = jnp.mean(acc, axis=-1, keepdims=True)
    xc = acc - mu
    var = jnp.mean(xc * xc, axis=-1, keepdims=True)
    y = xc * lax.rsqrt(var + LN_EPS) * lng + lnb
    return y * jax.nn.sigmoid(y)


def _conv_sample_kernel(state_ref, ua_ref, ug_ref, cw_ref, cb_ref, lng_ref, lnb_ref, sza_ref, sga_ref, wpa_ref,
                        out_ref, g_ref, hs_ref):
    n_state, nb, _ = state_ref.shape
    t_new = ua_ref.shape[0]
    g_ref[...] = ua_ref[...] * jax.nn.sigmoid(ug_ref[...])
    window = lambda r: state_ref[r] if r < n_state else g_ref[r - n_state]
    for t in range(t_new):
        acc = jnp.broadcast_to(cb_ref[...], (nb, cb_ref.shape[-1]))
        for j in range(CONV_WIDTH):
            acc = acc + cw_ref[j:j + 1, :] * window(t + j)
        h = _ln_swish(acc, lng_ref[...], lnb_ref[...]) * sza_ref[t * nb:(t + 1) * nb, :]
        hs_ref[t * nb:(t + 1) * nb, :] = h.astype(BF16)
    out_ref[...] = sga_ref[...] * jnp.dot(hs_ref[...], wpa_ref[...], preferred_element_type=F32)


def _conv_sample(state_tm, ua_tm, ug_tm, conv_w, conv_b, lng, lnb, sza_tm, sga_tm, wpa_bf):
    t_new, nb, c = ua_tm.shape
    d = wpa_bf.shape[1]
    assert state_tm.shape[0] == CONV_WIDTH - 1
    return pl.pallas_call(
        _conv_sample_kernel,
        out_shape=[jax.ShapeDtypeStruct((t_new * nb, d), F32), jax.ShapeDtypeStruct((t_new, nb, c), F32)],
        scratch_shapes=[pltpu.VMEM((t_new * nb, c), BF16)],
        compiler_params=pltpu.CompilerParams(vmem_limit_bytes=32 * MIB),
        name="conv_sample",
    )(state_tm, ua_tm, ug_tm, conv_w, conv_b, lng, lnb, sza_tm, sga_tm, wpa_bf)


def _lambda_value(lq1_ref, lk1_ref, lq2_ref, lk2_ref):
    s1 = jnp.sum(lq1_ref[...] * lk1_ref[...], axis=-1, keepdims=True)
    s2 = jnp.sum(lq2_ref[...] * lk2_ref[...], axis=-1, keepdims=True)
    return jnp.exp(s1) - jnp.exp(s2) + LAM_INIT


def _subln_gate(o, sub, szb):
    r = o * lax.rsqrt(jnp.mean(o * o, axis=-1, keepdims=True) + LN_EPS) * sub * (1.0 - LAM_INIT)
    return (r * szb).astype(BF16)


def _lane_blocks(x):
    return [x[:, i * LANES:(i + 1) * LANES] for i in range(x.shape[1] // LANES)]


def _block_max(s):
    return functools.reduce(jnp.maximum, _lane_blocks(s))


def _softmax_update(s, bmax, m_ref, l_ref, acc_ref, pv_fn):
    m_old = m_ref[...]
    m_new = jnp.maximum(m_old, jnp.max(bmax, axis=-1, keepdims=True))
    alpha = jnp.exp2(m_old - m_new)
    p_blocks = [jnp.exp2(sb - m_new) for sb in _lane_blocks(s)]
    l_ref[...] = alpha * l_ref[...] + functools.reduce(jnp.add, p_blocks)
    pv = pv_fn(jnp.concatenate(p_blocks, axis=1).astype(BF16))
    acc_ref[...] = jnp.concatenate([alpha] * (pv.shape[1] // LANES), axis=1) * acc_ref[...] + pv
    m_ref[...] = m_new


def _bias_by_distance(rel_bias, n_max):
    n = jnp.arange(n_max)
    max_exact = N_BUCKETS // 2
    large = max_exact + (jnp.log(jnp.maximum(n, 1).astype(F32) / max_exact)
                         / math.log(MAX_DISTANCE / max_exact) * (N_BUCKETS - max_exact)).astype(jnp.int32)
    large = jnp.minimum(large, N_BUCKETS - 1)
    bucket = jnp.where(n < max_exact, n, large)
    return jnp.transpose(rel_bias[bucket].astype(F32), (1, 0))


def _toeplitz_blocks(bdl):
    assert MAX_DISTANCE <= LANES
    per = 2 * LANES
    n = np.arange(per)
    blocks = []
    for e in range(2):
        idx = np.where(n < LANES, np.maximum(LANES * e - n, 0), LANES * e + per - n)
        w = bdl[:, idx]
        x = jnp.tile(w, (1, LANES))[:, :LANES * (per - 1)].reshape(-1, LANES, per - 1)
        blocks.append(x[:, :, :LANES])
    return jnp.stack(blocks, axis=1)


def _attn_prompt_kernel(cfar_ref, q_ref, k_ref, v_ref, u_ref, szb_ref, sub_ref, lq1_ref, lk1_ref, lq2_ref, lk2_ref,
                        out_ref, strip_ref, s_ref, bm_ref, m_ref, l_ref, acc_ref, *, tq, tk):
    h = pl.program_id(1)
    qi = pl.program_id(2)
    far = tq + 2 * LANES

    @pl.when(qi == 0)
    def _():
        strip_ref[...] = jnp.full(strip_ref.shape, cfar_ref[h], F32)
        strip_ref[tq:tq + LANES, :] = u_ref[0, 0]
        strip_ref[tq + LANES:far, :] = u_ref[0, 1]

    q = q_ref[0]
    qs = (q[:, :HEAD_DIM], q[:, HEAD_DIM:])
    m_ref[...] = jnp.full(m_ref.shape, NEG_INF, F32)
    l_ref[...] = jnp.zeros(l_ref.shape, F32)
    acc_ref[...] = jnp.zeros(acc_ref.shape, F32)

    def scores(j, slot):
        off = pl.multiple_of(j * tk, tk)
        kj = k_ref[0, pl.ds(off, tk), :]
        r0 = tq + (qi - j) * tk
        bias = [strip_ref[pl.ds(pl.multiple_of(jnp.minimum(r0 - cb * LANES, far), LANES), tq), :]
                for cb in range(tk // LANES)]
        for c in range(2):
            qk = lax.dot_general(qs[c], kj[:, c * HEAD_DIM:(c + 1) * HEAD_DIM], (((1,), (1,)), ((), ())),
                                 preferred_element_type=F32)
            s_blocks = [sb + bb for sb, bb in zip(_lane_blocks(qk), bias)]
            s_ref[slot, c] = jnp.concatenate(s_blocks, axis=1)
            bm_ref[slot, c] = functools.reduce(jnp.maximum, s_blocks)

    def consume(j, slot, mask):
        off = pl.multiple_of(j * tk, tk)
        vj = v_ref[0, pl.ds(off, tk), :]
        for c in range(2):
            s = s_ref[slot, c]
            if mask is None:
                bmax = bm_ref[slot, c]
            else:
                s = jnp.where(mask, s, NEG_INF)
                bmax = _block_max(s)
            _softmax_update(s, bmax, m_ref.at[c], l_ref.at[c], acc_ref.at[c],
                            lambda p: jnp.dot(p, vj, preferred_element_type=F32))

    scores(0, 0)

    def body(j, carry):
        consume(j, j % 2, None)
        scores(j + 1, (j + 1) % 2)
        return carry

    lax.fori_loop(0, qi, body, 0)

    row = lax.broadcasted_iota(jnp.int32, (tq, tk), 0)
    col = lax.broadcasted_iota(jnp.int32, (tq, tk), 1)
    consume(qi, qi % 2, col <= row)

    lam = _lambda_value(lq1_ref, lk1_ref, lq2_ref, lk2_ref)
    o1 = acc_ref[0] / jnp.sum(l_ref[0], axis=-1, keepdims=True)
    o2 = acc_ref[1] / jnp.sum(l_ref[1], axis=-1, keepdims=True)
    out_ref[0] = _subln_gate(o1 - lam * o2, sub_ref[...], szb_ref[0])


def _attn_prompt(q_bf, k_bf, v_bf, szb, bdl, sub, lq1, lk1, lq2, lk2):
    b, t, _ = q_bf.shape
    tq, tk = ATT_TQ, ATT_TK
    assert tq == tk and tq % LANES == 0 and t % tq == 0
    u = _toeplitz_blocks(bdl)
    cfar = bdl[:, MAX_DISTANCE]
    smem = pl.BlockSpec(memory_space=pltpu.SMEM)
    vec = pl.BlockSpec((1, HEAD_DIM), lambda bi, h, i: (0, 0))
    strip_rows = 2 * tq + 2 * LANES
    vmem = (2 * (2 * t * V_DIM * 2 + 2 * LANES * LANES * 4 + tq * V_DIM * (2 + 4 + 2))
            + 2 * 2 * tq * tk * 4 + (strip_rows + (2 * 2 + 2 * 2) * tq) * LANES * 4 + 2 * tq * V_DIM * 4
            + 4 * tq * tk * 4)
    return pl.pallas_call(
        functools.partial(_attn_prompt_kernel, tq=tq, tk=tk),
        grid=(b, N_HEADS, t // tq),
        in_specs=[
            smem,
            pl.BlockSpec((1, tq, V_DIM), lambda bi, h, i: (bi, i, h)),
            pl.BlockSpec((1, t, V_DIM), lambda bi, h, i: (bi, 0, h)),
            pl.BlockSpec((1, t, V_DIM), lambda bi, h, i: (bi, 0, h)),
            pl.BlockSpec((1, 2, LANES, LANES), lambda bi, h, i: (h, 0, 0, 0)),
            pl.BlockSpec((1, tq, V_DIM), lambda bi, h, i: (bi, i, h)),
            pl.BlockSpec((1, V_DIM), lambda bi, h, i: (0, 0)),
            vec, vec, vec, vec,
        ],
        out_specs=pl.BlockSpec((1, tq, V_DIM), lambda bi, h, i: (bi, i, h)),
        out_shape=jax.ShapeDtypeStruct((b, t, N_HEADS * V_DIM), BF16),
        scratch_shapes=[pltpu.VMEM((strip_rows, LANES), F32), pltpu.VMEM((2, 2, tq, tk), F32),
                        pltpu.VMEM((2, 2, tq, LANES), F32), pltpu.VMEM((2, tq, LANES), F32),
                        pltpu.VMEM((2, tq, LANES), F32), pltpu.VMEM((2, tq, V_DIM), F32)],
        compiler_params=_cparams(("arbitrary", "arbitrary", "arbitrary"), vmem // MIB + 6),
        name="attn_prompt",
    )(cfar, q_bf, k_bf, v_bf, u, szb, sub, lq1, lk1, lq2, lk2)


def _conv_rows(hist, cur, xs_ref, cw_ref, cb_ref, lng_ref, lnb_ref, sza):
    rb = cur.shape[0]
    n = 2 * rb
    xs_ref[0, 0:rb, :] = hist
    xs_ref[0, rb:n, :] = cur
    for r in range(1, V7X_SUBLANES):
        xs_ref[r, 0:n - V7X_SUBLANES, :] = xs_ref[0, r:r + n - V7X_SUBLANES, :]
    first = rb - (CONV_WIDTH - 1)
    acc = jnp.broadcast_to(cb_ref[...], cur.shape)
    for j in range(CONV_WIDTH):
        r, a = (first + j) % V7X_SUBLANES, (first + j) // V7X_SUBLANES
        acc = acc + cw_ref[j:j + 1, :] * xs_ref[r, V7X_SUBLANES * a:V7X_SUBLANES * a + rb, :]
    return (_ln_swish(acc, lng_ref[...], lnb_ref[...]) * sza).astype(BF16)


def _attn_decode_kernel(pt_ref, qc_ref, bfar_ref, bnear_ref, knew_ref, vnew_ref, bnew_ref, szb_ref, sub_ref,
                        lq1_ref, lk1_ref, lq2_ref, lk2_ref, gcur_ref, gprev_ref, sza_ref, cw_ref, cb_ref, lng_ref, lnb_ref,
                        *refs, n_pages, t_new, conv_blocks_per_seq):
    del pt_ref
    k_refs, v_refs = refs[:n_pages], refs[n_pages:2 * n_pages]
    out_ref, h_ref, m_ref, l_ref, acc_ref, xs_ref = refs[2 * n_pages:]
    s_idx = pl.program_id(1)

    last = s_idx == pl.num_programs(1) - 1
    nrow = qc_ref.shape[2]
    cols = k_refs[0].shape[1] // 2
    n_near = bnear_ref.shape[0]
    nt = (((1,), (1,)), ((), ()))
    qcs = (qc_ref[0, 0], qc_ref[0, 1])

    @pl.when(s_idx == 0)
    def _():
        m_ref[...] = jnp.full(m_ref.shape, NEG_INF, F32)
        l_ref[...] = jnp.zeros(l_ref.shape, F32)
        acc_ref[...] = jnp.zeros(acc_ref.shape, F32)

    bfar = bfar_ref[...]
    bias = [bfar] * (n_pages - n_near) + [jnp.where(last, bnear_ref[i], bfar) for i in range(n_near)]
    group = max(n_pages // DEC_GROUPS, 1)
    for lo in range(0, n_pages, group):
        pages = range(lo, min(lo + group, n_pages))
        s_maps = []
        for c in range(2):
            parts = [lax.dot_general(qcs[c], k_refs[i][0, pl.ds(c, cols, stride=2), :].astype(BF16), nt,
                                     preferred_element_type=F32) + bias[i] for i in pages]
            s_maps.append(jnp.concatenate(parts, axis=1))
        s_grp = jnp.concatenate(s_maps, axis=0)

        def pv_grp(p, pages=pages):
            out = None
            for n, i in enumerate(pages):
                term = jnp.dot(p[:, n * cols:(n + 1) * cols], v_refs[i][0].astype(BF16), preferred_element_type=F32)
                out = term if out is None else out + term
            return out

        _softmax_update(s_grp, _block_max(s_grp), m_ref, l_ref, acc_ref, pv_grp)

    blk = pl.program_id(0) * pl.num_programs(1) + s_idx
    hist = jnp.where(blk % conv_blocks_per_seq == 0, 0.0, gprev_ref[0])
    h_ref[0] = _conv_rows(hist, gcur_ref[0], xs_ref, cw_ref, cb_ref, lng_ref, lnb_ref, sza_ref[0])

    @pl.when(last)
    def _():
        s_new = jnp.concatenate([lax.dot_general(qcs[c], knew_ref[0, c], nt, preferred_element_type=F32) + bnew_ref[...]
                                 for c in range(2)], axis=0)
        row = lax.broadcasted_iota(jnp.int32, s_new.shape, 0)
        col = lax.broadcasted_iota(jnp.int32, s_new.shape, 1)
        valid = (col % N_HEADS == (row % nrow) // t_new) & (col // N_HEADS <= row % t_new)
        s_new = jnp.where(valid, s_new, NEG_INF)
        _softmax_update(s_new, _block_max(s_new), m_ref, l_ref, acc_ref,
                        lambda p: jnp.dot(p, vnew_ref[0], preferred_element_type=F32))

        lam = _lambda_value(lq1_ref, lk1_ref, lq2_ref, lk2_ref)
        o = acc_ref[...] / jnp.sum(l_ref[...], axis=-1, keepdims=True)
        out_ref[0] = _subln_gate(o[0:nrow] - lam * o[nrow:2 * nrow], sub_ref[...], szb_ref[0])


def _decode_bias(bdl, past, page, tn):
    nrow = N_HEADS * tn
    same_head = jnp.eye(N_HEADS, dtype=bool)[:, None, None, :]
    n_far = (past - MAX_DISTANCE + 1) // page
    n_near = past // page - n_far
    far = jnp.broadcast_to(jnp.where(same_head, bdl[:, MAX_DISTANCE][:, None, None, None], NEG_INF),
                           (N_HEADS, tn, page, N_HEADS)).reshape(nrow, page * N_HEADS)
    span = n_near * page
    near = jnp.stack([jnp.flip(bdl[:, qi + 1:qi + 1 + span], axis=1) for qi in range(tn)], axis=1)
    near = jnp.where(same_head, near[..., None], NEG_INF).reshape(nrow, n_near, page * N_HEADS)
    new_tok = LANES // N_HEADS
    dist_new = np.maximum(np.arange(tn)[:, None] - np.arange(new_tok)[None, :], 0)
    new = jnp.where(same_head, bdl[:, dist_new][..., None], 0.0).reshape(nrow, LANES)
    return far, jnp.transpose(near, (1, 0, 2)), new


def _attn_decode(page_table, cache_kr, cache_vr, qc, bias_far, bias_near, k_new, v_new, bias_new, szb, sub,
                 lq1, lk1, lq2, lk2, g, sza, conv_w, conv_b, clng, clnb):
    nb, n_pt = page_table.shape
    _, k_rows, _ = cache_kr.shape
    _, v_rows, _ = cache_vr.shape
    nrow = qc.shape[2]
    t_new = nrow // N_HEADS
    npg = DEC_PAGES
    assert n_pt % npg == 0 and k_rows == 2 * v_rows and bias_near.shape[0] <= npg
    new_rows = k_new.shape[2]

    def page_spec(rows, width, i):
        return pl.BlockSpec((1, rows, width), lambda b, s, pt, i=i: (pt[b, s * npg + i], 0, 0))

    ns = n_pt // npg
    bp, t, c = g.shape
    rb = bp * t // (nb * ns)
    assert rb * nb * ns == bp * t and rb % V7X_SUBLANES == 0 and rb >= CONV_WIDTH - 1 and t % rb == 0
    cps = t // rb
    cblk = lambda b, s, pt: ((b * ns + s) // cps, (b * ns + s) % cps, 0)
    cprev = lambda b, s, pt: ((b * ns + s) // cps, jnp.maximum((b * ns + s) % cps - 1, 0), 0)
    cconst = lambda shape: pl.BlockSpec(shape, lambda b, s, pt: (0,) * len(shape))
    vec = pl.BlockSpec((1, HEAD_DIM), lambda b, s, pt: (0, 0))
    in_specs = [
        pl.BlockSpec((1, 2, nrow, HEAD_DIM), lambda b, s, pt: (b, 0, 0, 0)),
        pl.BlockSpec(bias_far.shape, lambda b, s, pt: (0, 0)),
        pl.BlockSpec(bias_near.shape, lambda b, s, pt: (0, 0, 0)),
        pl.BlockSpec((1, 2, new_rows, HEAD_DIM), lambda b, s, pt: (b, 0, 0, 0)),
        pl.BlockSpec((1, new_rows, V_DIM), lambda b, s, pt: (b, 0, 0)),
        pl.BlockSpec((nrow, new_rows), lambda b, s, pt: (0, 0)),
        pl.BlockSpec((1, nrow, V_DIM), lambda b, s, pt: (b, 0, 0)),
        pl.BlockSpec((1, V_DIM), lambda b, s, pt: (0, 0)),
        vec, vec, vec, vec,
        pl.BlockSpec((1, rb, c), cblk), pl.BlockSpec((1, rb, c), cprev), pl.BlockSpec((1, rb, c), cblk),
        cconst((CONV_WIDTH, c)), cconst((1, c)), cconst((1, c)), cconst((1, c)),
    ] + [page_spec(k_rows, HEAD_DIM, i) for i in range(npg)] + [page_spec(v_rows, V_DIM, i) for i in range(npg)]
    page_bytes = k_rows * HEAD_DIM * 4
    vmem = 2 * 2 * npg * page_bytes + 10 * 2 * nrow * npg * v_rows * 4 + 24 * rb * c * 4
    return pl.pallas_call(
        functools.partial(_attn_decode_kernel, n_pages=npg, t_new=t_new, conv_blocks_per_seq=cps),
        grid_spec=pltpu.PrefetchScalarGridSpec(
            num_scalar_prefetch=1,
            grid=(nb, ns),
            in_specs=in_specs,
            out_specs=[pl.BlockSpec((1, nrow, V_DIM), lambda b, s, pt: (b, 0, 0)), pl.BlockSpec((1, rb, c), cblk)],
            scratch_shapes=[pltpu.VMEM((2 * nrow, LANES), F32), pltpu.VMEM((2 * nrow, LANES), F32),
                            pltpu.VMEM((2 * nrow, V_DIM), F32), pltpu.VMEM((V7X_SUBLANES, 2 * rb, c), F32)],
        ),
        out_shape=[jax.ShapeDtypeStruct((nb, nrow, V_DIM), BF16), jax.ShapeDtypeStruct((bp, t, c), BF16)],
        compiler_params=_cparams(("arbitrary", "arbitrary"), vmem // MIB + 6),
        name="attn_decode",
    )(page_table, qc, bias_far, bias_near, k_new, v_new, bias_new, szb, sub, lq1, lk1, lq2, lk2,
      g, g, sza, conv_w, conv_b, clng, clnb,
      *([cache_kr] * npg), *([cache_vr] * npg))


def _final_kernel(hb_ref, ma_ref, sgb_ref, x_ref, p_ref, wpb_ref, wout_ref, wgate_ref, wple_ref, lng_ref, lnb_ref, y_ref):
    b_proj = jnp.dot(hb_ref[...], wpb_ref[...], preferred_element_type=F32)
    merged = ma_ref[...] + sgb_ref[...] * b_proj
    z = ALPHA * x_ref[...] + jnp.dot(merged.astype(BF16), wout_ref[...], preferred_element_type=F32)
    mu = jnp.mean(z, axis=-1, keepdims=True)
    zc = z - mu
    var = jnp.mean(zc * zc, axis=-1, keepdims=True)
    h = zc * lax.rsqrt(var + LN_EPS) * lng_ref[...] + lnb_ref[...]
    gate = jax.nn.sigmoid(jnp.dot(h.astype(BF16), wgate_ref[...], preferred_element_type=F32))
    pe = jnp.dot(p_ref[...].astype(BF16), wple_ref[...], preferred_element_type=F32)
    y_ref[...] = h + gate * pe


def _final(hb, ma, sgb, sgb_block, x, p, wpb_bf, wout_bf, wgate_bf, wple_bf, lng, lnb, tm):
    m, d = x.shape
    pd = p.shape[1]
    row = lambda w: pl.BlockSpec((tm, w), lambda i: (i, 0))
    const = lambda r, c: pl.BlockSpec((r, c), lambda i: (0, 0), pipeline_mode=pl.Buffered(1))
    vmem = 2 * tm * (d * 2 + 4 * d * 4 + pd * 4) + (3 * d * d + pd * d) * 2 + 6 * tm * d * 4
    return pl.pallas_call(
        _final_kernel,
        grid=(m // tm,),
        in_specs=[row(d), row(d), pl.BlockSpec((tm, d), lambda i: (i, sgb_block)), row(d), row(pd),
                  const(d, d), const(d, d), const(d, d), const(pd, d), const(1, d), const(1, d)],
        out_specs=row(d),
        out_shape=jax.ShapeDtypeStruct((m, d), F32),
        compiler_params=_cparams(("arbitrary",), vmem // MIB + 6),
        name=f"final_m{m}",
    )(hb, ma, sgb, x, p, wpb_bf, wout_bf, wgate_bf, wple_bf, lng, lnb)


def kernel(x_prompt, x_sample, p_prompt, p_sample, cache_k, cache_v, state_conv, page_table, w_in, conv_w, conv_b, conv_ln_g, conv_ln_b, w_proj_a, lambda_q1, lambda_k1, lambda_q2, lambda_k2, subln_w, w_proj_b, w_out, ln_g, ln_b, w_ple_proj, w_ple_gate, rel_bias):
    assert w_in.shape[0] == DEPTH
    b, t, d = x_prompt.shape
    nb, tn, _ = x_sample.shape
    c = conv_w.shape[-1]
    n_maps = 2 * N_HEADS
    qk_w = n_maps * HEAD_DIM
    att_w = N_HEADS * V_DIM
    pd = p_prompt.shape[-1]
    n_phys, page = cache_k.shape[1], cache_k.shape[2]
    past = page_table.shape[1] * page

    wpa_bf = w_proj_a[0].astype(BF16)
    wpb_bf = w_proj_b[0].astype(BF16)
    wout_bf = w_out[0].astype(BF16)
    wgate_bf = w_ple_gate[0].astype(BF16)
    wple_bf = w_ple_proj[0].astype(BF16)
    cw, cb = conv_w[0], conv_b[0].reshape(1, c)
    clng, clnb = conv_ln_g[0].reshape(1, c), conv_ln_b[0].reshape(1, c)
    lng, lnb = ln_g[0].reshape(1, d), ln_b[0].reshape(1, d)
    sub = subln_w[0].reshape(1, V_DIM)
    lq1, lk1, lq2, lk2 = (v[0].reshape(1, HEAD_DIM) for v in (lambda_q1, lambda_k1, lambda_q2, lambda_k2))
    bdl = _bias_by_distance(rel_bias, MAX_DISTANCE + 2 * LANES + page + tn) * LOG2E

    ms = nb * tn
    xs = jnp.transpose(x_sample, (1, 0, 2)).reshape(ms, d)
    w_bf, (ua_s, ug_s, sza_s, q_s, k_s, k_s_bf, v_s, v_s_bf, szb_s, sga_s, sgb_s) = _in_projection_small(
        xs.astype(BF16), w_in[0], c, qk_w, att_w, d)

    xp = x_prompt.reshape(b * t, d)
    xp_bf, g, sza, q_bf, k_p, k_bf, v_p, v_bf, szb, sgb, ga_col0 = _in_projection(xp, w_bf, PROJ_TM, c, qk_w, att_w, d)
    g3 = g.reshape(b, t, c)
    hb = _attn_prompt(q_bf.reshape(b, t, qk_w), k_bf.reshape(b, t, qk_w), v_bf.reshape(b, t, att_w),
                      szb.reshape(b, t, att_w), bdl, sub, lq1, lk1, lq2, lk2)

    ma_s, g_s_tm = _conv_sample(jnp.transpose(state_conv[0], (1, 0, 2)), ua_s.reshape(tn, nb, c), ug_s.reshape(tn, nb, c),
                                cw, cb, clng, clnb, sza_s, sga_s, wpa_bf)

    nrow = N_HEADS * tn
    heads = lambda a, w: a.reshape(tn, nb, N_HEADS, *w)
    qc = jnp.transpose(heads(q_s, (2, HEAD_DIM)), (1, 3, 2, 0, 4)).reshape(nb, 2, nrow, HEAD_DIM)
    k_new = jnp.transpose(heads(k_s_bf, (2, HEAD_DIM)), (1, 3, 0, 2, 4)).reshape(nb, 2, nrow, HEAD_DIM)
    k_new = jnp.pad(k_new, ((0, 0), (0, 0), (0, LANES - nrow), (0, 0)))
    v_new = jnp.transpose(heads(v_s_bf, (V_DIM,)), (1, 0, 2, 3)).reshape(nb, nrow, V_DIM)
    v_new = jnp.pad(v_new, ((0, 0), (0, LANES - nrow), (0, 0)))
    szb_hq = jnp.transpose(heads(szb_s, (V_DIM,)), (1, 2, 0, 3)).reshape(nb, nrow, V_DIM)
    bias_far, bias_near, bias_new = _decode_bias(bdl, past, page, tn)
    hq, h_conv = _attn_decode(page_table, cache_k.reshape(n_phys, page * n_maps, HEAD_DIM),
                              cache_v.reshape(n_phys, page * N_HEADS, V_DIM), qc, bias_far, bias_near, k_new, v_new,
                              bias_new, szb_hq, sub, lq1, lk1, lq2, lk2, g3, sza.reshape(b, t, c), cw, cb, clng, clnb)
    hb_s = jnp.transpose(hq.reshape(nb, N_HEADS, tn, V_DIM), (2, 0, 1, 3)).reshape(ms, att_w)
    y_s = _final(hb_s, ma_s, sgb_s, 0, xs, jnp.transpose(p_sample[0], (1, 0, 2)).reshape(ms, pd),
                 wpb_bf, wout_bf, wgate_bf, wple_bf, lng, lnb, ms)

    ma = _proj_ga(xp_bf, w_bf, ga_col0, h_conv.reshape(b * t, c), wpa_bf, PROJ_TM)
    y_p = _final(hb.reshape(b * t, att_w), ma, sgb, 0, xp, p_prompt[0].reshape(b * t, pd),
                 wpb_bf, wout_bf, wgate_bf, wple_bf, lng, lnb, FINAL_TM)

    batch_major = lambda a: jnp.transpose(a.reshape(tn, nb, *a.shape[1:]), (1, 0) + tuple(range(2, a.ndim + 1)))
    conv_prompt = g3[:, t - (CONV_WIDTH - 1):, :]
    conv_sample = jnp.concatenate([state_conv[0][:, tn:, :], jnp.transpose(g_s_tm, (1, 0, 2))], axis=1)
    return (y_p.reshape(b, t, d), batch_major(y_s),
            k_p.reshape(1, b, t, n_maps, HEAD_DIM), v_p.reshape(1, b, t, N_HEADS, V_DIM), conv_prompt[None],
            batch_major(k_s)[None], batch_major(v_s.reshape(ms, N_HEADS, V_DIM))[None], conv_sample[None])
```

```python
import functools
import math

import numpy as np
import jax
import jax.numpy as jnp
from jax import lax
from jax.experimental import pallas as pl
from jax.experimental.pallas import tpu as pltpu

F32 = jnp.float32
BF16 = jnp.bfloat16

N_HEADS = 8
HEAD_DIM = 128
V_DIM = 2 * HEAD_DIM
CONV_WIDTH = 31
N_BUCKETS = 32
MAX_DISTANCE = 128
LN_EPS = 1e-5
NEG_INF = -1e30
DEPTH = 1
ALPHA = (2 * DEPTH) ** 0.25
LAM_INIT = 0.8 - 0.6 * math.exp(-0.3 * 0)
QK_SCALE = HEAD_DIM ** -0.5
LOG2E = math.log2(math.e)

V7X_VMEM_BYTES = 64 * 1024 * 1024
V7X_SUBLANES = 8
BF16_ROWS = 2 * V7X_SUBLANES
LANES = 128
MIB = 1024 * 1024
VMEM_HEADROOM_BYTES = 6 * MIB

PROJ_TN = 1024
PROJ_TM = 1024
GLU_TM = 512
ATT_TQ = 512
ATT_TK = 512
DEC_PAGES = 8
DEC_GROUPS = 4
FINAL_TM = 256


def _cparams(sem, block_bytes):
    limit = min(block_bytes + VMEM_HEADROOM_BYTES, V7X_VMEM_BYTES - VMEM_HEADROOM_BYTES)
    return pltpu.CompilerParams(dimension_semantics=sem, vmem_limit_bytes=limit)


def _proj_kernel(x_ref, *refs, n_w, epilogue):
    w_refs, out_refs = refs[:n_w], refs[n_w:]
    x = x_ref[...]
    accs = [jnp.dot(x, w[...], preferred_element_type=F32) for w in w_refs]
    for o_ref, o in zip(out_refs, epilogue(*accs)):
        o_ref[...] = o.astype(o_ref.dtype).reshape(o_ref.shape)


def _proj(name, x_bf, w_bf, col_starts, ncols, epilogue, out_dtypes, tm, head_width=None):
    m, k = x_bf.shape
    tn = PROJ_TN
    n_w = len(col_starts)
    in_specs = [pl.BlockSpec((tm, k), lambda j, i: (i, 0))]
    for c0 in col_starts:
        assert c0 % tn == 0
        in_specs.append(pl.BlockSpec((k, tn), lambda j, i, c0=c0: (0, c0 // tn + j)))
    out_specs = [pl.BlockSpec((tm, tn), lambda j, i: (i, j)) for _ in out_dtypes]
    out_shape = [jax.ShapeDtypeStruct((m, ncols), dt) for dt in out_dtypes]
    if head_width is not None:
        assert (tn // head_width) % V7X_SUBLANES == 0
        out_specs[0] = pl.BlockSpec((tm, tn // head_width, head_width), lambda j, i: (i, j, 0))
        out_shape[0] = jax.ShapeDtypeStruct((m, ncols // head_width, head_width), out_dtypes[0])
    vmem = 2 * (tm * k * 2 + n_w * k * tn * 2 + sum(tm * tn * jnp.dtype(d).itemsize for d in out_dtypes))
    vmem += n_w * tm * tn * 4
    return pl.pallas_call(
        functools.partial(_proj_kernel, n_w=n_w, epilogue=epilogue),
        grid=(ncols // tn, m // tm),
        in_specs=in_specs, out_specs=out_specs, out_shape=out_shape,
        compiler_params=_cparams(("arbitrary", "arbitrary"), vmem),
        name=f"proj_{name}_m{m}",
    )(x_bf, *([w_bf] * n_w))


def _sigmoid(z):
    return 0.5 * jnp.tanh(0.5 * z) + 0.5


def _silu(z):
    return z * _sigmoid(z)


def _ep_silu(z):
    return (_silu(z),)


def _ep_sigmoid(z):
    return (_sigmoid(z),)


def _ep_query(z):
    return (z * (QK_SCALE * LOG2E),)


def _ep_copy2(z):
    return (z, z)


def _glu_cast_kernel(x_ref, wa_ref, wg_ref, wz_ref, g_ref, sza_ref, xbf_ref):
    xb = x_ref[...].astype(BF16)
    xbf_ref[...] = xb
    ua = jnp.dot(xb, wa_ref[...], preferred_element_type=F32)
    ug = jnp.dot(xb, wg_ref[...], preferred_element_type=F32)
    g_ref[...] = ua * _sigmoid(ug)
    sza_ref[...] = _silu(jnp.dot(xb, wz_ref[...], preferred_element_type=F32)).astype(sza_ref.dtype)


def _proj_glu_cast(x_f32, w_bf, c_conv):
    m, k = x_f32.shape
    tm, tn = GLU_TM, PROJ_TN
    assert c_conv == tn and m % tm == 0
    vmem = 2 * (tm * k * 4 + 3 * k * tn * 2 + 2 * tm * tn * 4 + tm * k * 2) + 3 * tm * tn * 4 + tm * k * 2
    wspec = lambda n: pl.BlockSpec((k, tn), lambda i: (0, n))
    return pl.pallas_call(
        _glu_cast_kernel,
        grid=(m // tm,),
        in_specs=[pl.BlockSpec((tm, k), lambda i: (i, 0)), wspec(0), wspec(1), wspec(2)],
        out_specs=[pl.BlockSpec((tm, tn), lambda i: (i, 0)), pl.BlockSpec((tm, tn), lambda i: (i, 0)),
                   pl.BlockSpec((tm, k), lambda i: (i, 0))],
        out_shape=[jax.ShapeDtypeStruct((m, c_conv), F32), jax.ShapeDtypeStruct((m, c_conv), BF16),
                   jax.ShapeDtypeStruct((m, k), BF16)],
        compiler_params=_cparams(("arbitrary",), vmem),
        name="proj_glu_cast",
    )(x_f32, w_bf, w_bf, w_bf)


def _proj_value_kernel(x_ref, wa_ref, wb_ref, o_ref, obf_ref):
    x = x_ref[...]
    z = jnp.concatenate([jnp.dot(x, wa_ref[...], preferred_element_type=F32),
                         jnp.dot(x, wb_ref[...], preferred_element_type=F32)], axis=1)
    o_ref[...] = z.reshape(o_ref.shape)
    obf_ref[...] = z.astype(BF16)


def _proj_value(x_bf, w_bf, col0, att_w):
    m, k = x_bf.shape
    tm, tn = GLU_TM, PROJ_TN
    assert att_w == 2 * tn == N_HEADS * V_DIM and col0 % tn == 0 and m % tm == 0
    vmem = 2 * (tm * k * 2 + 2 * k * tn * 2 + tm * att_w * 6) + 2 * tm * att_w * 4
    return pl.pallas_call(
        _proj_value_kernel,
        grid=(m // tm,),
        in_specs=[pl.BlockSpec((tm, k), lambda i: (i, 0)), pl.BlockSpec((k, tn), lambda i: (0, col0 // tn)),
                  pl.BlockSpec((k, tn), lambda i: (0, col0 // tn + 1))],
        out_specs=[pl.BlockSpec((tm, N_HEADS, V_DIM), lambda i: (i, 0, 0)), pl.BlockSpec((tm, att_w), lambda i: (i, 0))],
        out_shape=[jax.ShapeDtypeStruct((m, N_HEADS, V_DIM), F32), jax.ShapeDtypeStruct((m, att_w), BF16)],
        compiler_params=_cparams(("arbitrary",), vmem),
        name="proj_value",
    )(x_bf, w_bf, w_bf)


def _in_projection(x_f32, w_bf, tm, c_conv, qk_w, att_w, d_model):
    g, sza, x_bf = _proj_glu_cast(x_f32, w_bf, c_conv)
    o = 3 * c_conv
    (q_bf,) = _proj("q", x_bf, w_bf, (o,), qk_w, _ep_query, (BF16,), tm)
    o += qk_w
    k, k_bf = _proj("k", x_bf, w_bf, (o,), qk_w, _ep_copy2, (F32, BF16), tm, head_width=HEAD_DIM)
    o += qk_w
    v, v_bf = _proj_value(x_bf, w_bf, o, att_w)
    o += att_w
    (szb,) = _proj("zb", x_bf, w_bf, (o,), att_w, _ep_silu, (F32,), tm)
    o += att_w
    (sgb,) = _proj("gb", x_bf, w_bf, (o + d_model,), d_model, _ep_sigmoid, (F32,), tm)
    return x_bf, g, sza, q_bf, k, k_bf, v, v_bf, szb, sgb, o


def _proj_ga_kernel(x_ref, w_ref, h_ref, wpa_ref, o_ref):
    gate = jax.nn.sigmoid(jnp.dot(x_ref[...], w_ref[...], preferred_element_type=F32))
    o_ref[...] = gate * jnp.dot(h_ref[...], wpa_ref[...], preferred_element_type=F32)


def _proj_ga(x_bf, w_bf, col0, h_bf, wpa_bf, tm):
    m, k = x_bf.shape
    c, d = wpa_bf.shape
    tn = PROJ_TN
    assert col0 % tn == 0 and d % tn == 0
    vmem = 2 * (tm * k * 2 + k * tn * 2 + tm * c * 2 + c * tn * 2 + tm * tn * 4) + 2 * tm * tn * 4
    return pl.pallas_call(
        _proj_ga_kernel,
        grid=(d // tn, m // tm),
        in_specs=[pl.BlockSpec((tm, k), lambda j, i: (i, 0)), pl.BlockSpec((k, tn), lambda j, i: (0, col0 // tn + j)),
                  pl.BlockSpec((tm, c), lambda j, i: (i, 0)), pl.BlockSpec((c, tn), lambda j, i: (0, j))],
        out_specs=pl.BlockSpec((tm, tn), lambda j, i: (i, j)),
        out_shape=jax.ShapeDtypeStruct((m, d), F32),
        compiler_params=_cparams(("arbitrary", "arbitrary"), vmem),
        name="proj_ga",
    )(x_bf, w_bf, h_bf, wpa_bf)


def _proj_small_kernel(x_ref, w_ref, *refs, groups):
    j = pl.program_id(0)
    wbf_ref, out_refs = refs[-1], refs[:-1]
    w = w_ref[...].astype(BF16)
    wbf_ref[...] = w
    z = jnp.dot(x_ref[...], w, preferred_element_type=F32)
    for o_ref, (lo, hi, fn) in zip(out_refs, groups):
        @pl.when((j >= lo) & (j < hi))
        def _(o_ref=o_ref, fn=fn):
            o_ref[...] = fn(z).astype(o_ref.dtype).reshape(o_ref.shape)


def _in_projection_small(x_bf, w_f32, c_conv, qk_w, att_w, d_model):
    m, k = x_bf.shape
    tn = PROJ_TN
    sizes = (c_conv, c_conv, c_conv, qk_w, qk_w, att_w, att_w, d_model, d_model)
    starts = np.concatenate([[0], np.cumsum(sizes)])
    assert all(s % tn == 0 for s in sizes)
    n_in = int(starts[-1])
    ident = lambda z: z
    query = lambda z: z * (QK_SCALE * LOG2E)
    outs = [(0, ident, F32, None), (1, ident, F32, None), (2, _silu, F32, None), (3, query, BF16, None),
            (4, ident, F32, HEAD_DIM), (4, ident, BF16, None), (5, ident, F32, None), (5, ident, BF16, None),
            (6, _silu, F32, None), (7, _sigmoid, F32, None), (8, _sigmoid, F32, None)]
    groups, out_specs, out_shape = [], [], []
    for grp, fn, dt, hw in outs:
        lo, hi = int(starts[grp]) // tn, int(starts[grp + 1]) // tn
        groups.append((lo, hi, fn))
        pin = lambda j, lo=lo, hi=hi: jnp.clip(j - lo, 0, hi - lo - 1)
        if hw is None:
            out_specs.append(pl.BlockSpec((m, tn), lambda j, pin=pin: (0, pin(j))))
            out_shape.append(jax.ShapeDtypeStruct((m, sizes[grp]), dt))
        else:
            assert (tn // hw) % V7X_SUBLANES == 0
            out_specs.append(pl.BlockSpec((m, tn // hw, hw), lambda j, pin=pin: (0, pin(j), 0)))
            out_shape.append(jax.ShapeDtypeStruct((m, sizes[grp] // hw, hw), dt))
    out_specs.append(pl.BlockSpec((k, tn), lambda j: (0, j)))
    out_shape.append(jax.ShapeDtypeStruct((k, n_in), BF16))
    vmem = 2 * (m * k * 2 + k * tn * 6 + len(outs) * m * tn * 4) + k * tn * 2 + 4 * m * tn * 4
    res = pl.pallas_call(
        functools.partial(_proj_small_kernel, groups=tuple(groups)),
        grid=(n_in // tn,),
        in_specs=[pl.BlockSpec((m, k), lambda j: (0, 0)), pl.BlockSpec((k, tn), lambda j: (0, j))],
        out_specs=out_specs, out_shape=out_shape,
        compiler_params=_cparams(("arbitrary",), vmem),
        name="proj_small",
    )(x_bf, w_f32)
    return res[-1], tuple(res[:-1])


def _ln_swish(acc, lng, lnb):
    mu = jnp.mean(acc, axis=-1, keepdims=True)
    xc = acc - mu
    var = jnp.mean(xc * xc, axis=-1, keepdims=True)
    y = xc * lax.rsqrt(var + LN_EPS) * lng + lnb
    return _silu(y)


def _conv_sample_kernel(state_ref, ua_ref, ug_ref, cw_ref, cb_ref, lng_ref, lnb_ref, sza_ref, sga_ref, wpa_ref,
                        out_ref, g_ref, hs_ref):
    n_state, nb, _ = state_ref.shape
    t_new = ua_ref.shape[0]
    g_ref[...] = ua_ref[...] * _sigmoid(ug_ref[...])
    window = lambda r: state_ref[r] if r < n_state else g_ref[r - n_state]
    for t in range(t_new):
        acc = jnp.broadcast_to(cb_ref[...], (nb, cb_ref.shape[-1]))
        for j in range(CONV_WIDTH):
            acc = acc + cw_ref[j:j + 1, :] * window(t + j)
        h = _ln_swish(acc, lng_ref[...], lnb_ref[...]) * sza_ref[t * nb:(t + 1) * nb, :]
        hs_ref[t * nb:(t + 1) * nb, :] = h.astype(BF16)
    out_ref[...] = sga_ref[...] * jnp.dot(hs_ref[...], wpa_ref[...], preferred_element_type=F32)


def _conv_sample(state_tm, ua_tm, ug_tm, conv_w, conv_b, lng, lnb, sza_tm, sga_tm, wpa_bf):
    t_new, nb, c = ua_tm.shape
    d = wpa_bf.shape[1]
    assert state_tm.shape[0] == CONV_WIDTH - 1
    return pl.pallas_call(
        _conv_sample_kernel,
        out_shape=[jax.ShapeDtypeStruct((t_new * nb, d), F32), jax.ShapeDtypeStruct((t_new, nb, c), F32)],
        scratch_shapes=[pltpu.VMEM((t_new * nb, c), BF16)],
        compiler_params=_cparams(None, 4 * (state_tm.size + 3 * ua_tm.size + 3 * sga_tm.size) + 2 * wpa_bf.size),
        name="conv_sample",
    )(state_tm, ua_tm, ug_tm, conv_w, conv_b, lng, lnb, sza_tm, sga_tm, wpa_bf)


def _lambda_value(lq1_ref, lk1_ref, lq2_ref, lk2_ref):
    s1 = jnp.sum(lq1_ref[...] * lk1_ref[...], axis=-1, keepdims=True)
    s2 = jnp.sum(lq2_ref[...] * lk2_ref[...], axis=-1, keepdims=True)
    return jnp.exp(s1) - jnp.exp(s2) + LAM_INIT


def _subln_gate(o, sub, szb):
    r = o * lax.rsqrt(jnp.mean(o * o, axis=-1, keepdims=True) + LN_EPS) * sub * (1.0 - LAM_INIT)
    return (r * szb).astype(BF16)


def _lane_blocks(x):
    return [x[:, i * LANES:(i + 1) * LANES] for i in range(x.shape[1] // LANES)]


def _block_max(s):
    return functools.reduce(jnp.maximum, _lane_blocks(s))


def _softmax_update(s, bmax, m_ref, l_ref, acc_ref, pv_fn):
    m_old = m_ref[...]
    m_new = jnp.maximum(m_old, jnp.max(bmax, axis=-1, keepdims=True))
    alpha = jnp.exp2(m_old - m_new)
    p_blocks = [jnp.exp2(sb - m_new) for sb in _lane_blocks(s)]
    l_ref[...] = alpha * l_ref[...] + functools.reduce(jnp.add, p_blocks)
    pv = pv_fn(jnp.concatenate(p_blocks, axis=1).astype(BF16))
    acc_ref[...] = jnp.concatenate([alpha] * (pv.shape[1] // LANES), axis=1) * acc_ref[...] + pv
    m_ref[...] = m_new


def _bias_by_distance(rel_bias, n_max):
    n = jnp.arange(n_max)
    max_exact = N_BUCKETS // 2
    large = max_exact + (jnp.log(jnp.maximum(n, 1).astype(F32) / max_exact)
                         / math.log(MAX_DISTANCE / max_exact) * (N_BUCKETS - max_exact)).astype(jnp.int32)
    large = jnp.minimum(large, N_BUCKETS - 1)
    bucket = jnp.where(n < max_exact, n, large)
    return jnp.transpose(rel_bias[bucket].astype(F32), (1, 0))


def _toeplitz_blocks(bdl):
    assert MAX_DISTANCE <= LANES
    per = 2 * LANES
    n = np.arange(per)
    blocks = []
    for e in range(2):
        idx = np.where(n < LANES, np.maximum(LANES * e - n, 0), LANES * e + per - n)
        w = bdl[:, idx]
        x = jnp.tile(w, (1, LANES))[:, :LANES * (per - 1)].reshape(-1, LANES, per - 1)
        blocks.append(x[:, :, :LANES])
    return jnp.stack(blocks, axis=1)


def _attn_prompt_kernel(cfar_ref, q_ref, k_ref, v_ref, u_ref, szb_ref, sub_ref, lq1_ref, lk1_ref, lq2_ref, lk2_ref,
                        *refs, tq, tk, n_cast):
    cast_in, (out_ref, *cast_out) = refs[:n_cast], refs[n_cast:2 * n_cast + 1]
    strip_ref, s_ref, bm_ref, m_ref, l_ref, acc_ref = refs[2 * n_cast + 1:]
    h = pl.program_id(0)
    qi = pl.program_id(1)
    nbt = q_ref.shape[0]
    far = tq + 2 * LANES

    @pl.when(qi == 0)
    def _():
        strip_ref[...] = jnp.full(strip_ref.shape, cfar_ref[h], F32)
        strip_ref[tq:tq + LANES, :] = u_ref[0, 0]
        strip_ref[tq + LANES:far, :] = u_ref[0, 1]

    m_ref[...] = jnp.full(m_ref.shape, NEG_INF, F32)
    l_ref[...] = jnp.zeros(l_ref.shape, F32)
    acc_ref[...] = jnp.zeros(acc_ref.shape, F32)

    def scores(j, slot):
        off = pl.multiple_of(j * tk, tk)
        r0 = tq + (qi - j) * tk
        bias = [strip_ref[pl.ds(pl.multiple_of(jnp.minimum(r0 - cb * LANES, far), LANES), tq), :]
                for cb in range(tk // LANES)]
        for bi in range(nbt):
            kj = k_ref[bi, pl.ds(off, tk), :]
            q = q_ref[bi]
            for c in range(2):
                qk = lax.dot_general(q[:, c * HEAD_DIM:(c + 1) * HEAD_DIM], kj[:, c * HEAD_DIM:(c + 1) * HEAD_DIM],
                                     (((1,), (1,)), ((), ())), preferred_element_type=F32)
                s_blocks = [sb + bb for sb, bb in zip(_lane_blocks(qk), bias)]
                s_ref[slot, bi, c] = jnp.concatenate(s_blocks, axis=1)
                bm_ref[slot, bi, c] = functools.reduce(jnp.maximum, s_blocks)

    def consume(j, slot, mask):
        off = pl.multiple_of(j * tk, tk)
        for bi in range(nbt):
            vj = v_ref[bi, pl.ds(off, tk), :]
            for c in range(2):
                s = s_ref[slot, bi, c]
                if mask is None:
                    bmax = bm_ref[slot, bi, c]
                else:
                    s = jnp.where(mask, s, NEG_INF)
                    bmax = _block_max(s)
                _softmax_update(s, bmax, m_ref.at[bi, c], l_ref.at[bi, c], acc_ref.at[bi, c],
                                lambda p, vj=vj: jnp.dot(p, vj, preferred_element_type=F32))

    scores(0, 0)

    def body(j, carry):
        consume(j, j % 2, None)
        scores(j + 1, (j + 1) % 2)
        return carry

    lax.fori_loop(0, qi, body, 0)

    row = lax.broadcasted_iota(jnp.int32, (tq, tk), 0)
    col = lax.broadcasted_iota(jnp.int32, (tq, tk), 1)
    consume(qi, qi % 2, col <= row)

    lam = _lambda_value(lq1_ref, lk1_ref, lq2_ref, lk2_ref)
    for bi in range(nbt):
        o1 = acc_ref[bi, 0] / jnp.sum(l_ref[bi, 0], axis=-1, keepdims=True)
        o2 = acc_ref[bi, 1] / jnp.sum(l_ref[bi, 1], axis=-1, keepdims=True)
        out_ref[bi] = _subln_gate(o1 - lam * o2, sub_ref[...], szb_ref[bi])

    for w_ref, o_ref in zip(cast_in, cast_out):
        o_ref[...] = w_ref[...].astype(BF16)


def _attn_prompt(q_bf, k_bf, v_bf, szb, bdl, sub, lq1, lk1, lq2, lk2, weights):
    b, t, _ = q_bf.shape
    tq, tk = ATT_TQ, ATT_TK
    assert tq == tk and tq % LANES == 0 and t % tq == 0
    nq = t // tq
    n_steps = N_HEADS * nq
    cast_specs = []
    for w in weights:
        rows = max(BF16_ROWS, -(-w.shape[0] // n_steps // BF16_ROWS) * BF16_ROWS)
        assert w.shape[0] % rows == 0 and w.shape[0] // rows <= n_steps
        last_blk = w.shape[0] // rows - 1
        cast_specs.append(pl.BlockSpec(
            (rows, w.shape[1]), lambda h, i, last_blk=last_blk: (jnp.minimum(h * nq + i, last_blk), 0)))
    u = _toeplitz_blocks(bdl)
    cfar = bdl[:, MAX_DISTANCE]
    smem = pl.BlockSpec(memory_space=pltpu.SMEM)
    vec = pl.BlockSpec((1, HEAD_DIM), lambda h, i: (0, 0))
    strip_rows = 2 * tq + 2 * LANES
    vmem = (2 * b * (2 * t * V_DIM * 2 + tq * V_DIM * (2 + 4 + 2)) + 4 * LANES * LANES * 4
            + b * 2 * 2 * tq * tk * 4 + (strip_rows + b * (2 * 2 + 2 * 2) * tq) * LANES * 4 + b * 2 * tq * V_DIM * 4
            + 4 * tq * tk * 4)
    return pl.pallas_call(
        functools.partial(_attn_prompt_kernel, tq=tq, tk=tk, n_cast=len(weights)),
        grid=(N_HEADS, nq),
        in_specs=[
            smem,
            pl.BlockSpec((b, tq, V_DIM), lambda h, i: (0, i, h)),
            pl.BlockSpec((b, t, V_DIM), lambda h, i: (0, 0, h)),
            pl.BlockSpec((b, t, V_DIM), lambda h, i: (0, 0, h)),
            pl.BlockSpec((1, 2, LANES, LANES), lambda h, i: (h, 0, 0, 0)),
            pl.BlockSpec((b, tq, V_DIM), lambda h, i: (0, i, h)),
            pl.BlockSpec((1, V_DIM), lambda h, i: (0, 0)),
            vec, vec, vec, vec,
        ] + cast_specs,
        out_specs=[pl.BlockSpec((b, tq, V_DIM), lambda h, i: (0, i, h))] + cast_specs,
        out_shape=[jax.ShapeDtypeStruct((b, t, N_HEADS * V_DIM), BF16)]
        + [jax.ShapeDtypeStruct(w.shape, BF16) for w in weights],
        scratch_shapes=[pltpu.VMEM((strip_rows, LANES), F32), pltpu.VMEM((2, b, 2, tq, tk), F32),
                        pltpu.VMEM((2, b, 2, tq, LANES), F32), pltpu.VMEM((b, 2, tq, LANES), F32),
                        pltpu.VMEM((b, 2, tq, LANES), F32), pltpu.VMEM((b, 2, tq, V_DIM), F32)],
        compiler_params=_cparams(("arbitrary", "arbitrary"), vmem),
        name="attn_prompt",
    )(cfar, q_bf, k_bf, v_bf, u, szb, sub, lq1, lk1, lq2, lk2, *weights)


def _conv_rows(hist, cur, xs_ref, cw_ref, cb_ref, lng_ref, lnb_ref, sza):
    rb = cur.shape[0]
    n = 2 * rb
    xs_ref[0, 0:rb, :] = hist
    xs_ref[0, rb:n, :] = cur
    for r in range(1, V7X_SUBLANES):
        xs_ref[r, 0:n - V7X_SUBLANES, :] = xs_ref[0, r:r + n - V7X_SUBLANES, :]
    first = rb - (CONV_WIDTH - 1)
    acc = jnp.broadcast_to(cb_ref[...], cur.shape)
    for j in range(CONV_WIDTH):
        r, a = (first + j) % V7X_SUBLANES, (first + j) // V7X_SUBLANES
        acc = acc + cw_ref[j:j + 1, :] * xs_ref[r, V7X_SUBLANES * a:V7X_SUBLANES * a + rb, :]
    return (_ln_swish(acc, lng_ref[...], lnb_ref[...]) * sza).astype(BF16)


def _attn_decode_kernel(pt_ref, qc_ref, bfar_ref, bnear_ref, knew_ref, vnew_ref, bnew_ref, szb_ref, sub_ref,
                        lq1_ref, lk1_ref, lq2_ref, lk2_ref, gcur_ref, gprev_ref, sza_ref, cw_ref, cb_ref, lng_ref, lnb_ref,
                        *refs, n_pages, t_new, conv_blocks_per_seq):
    del pt_ref
    k_refs, v_refs = refs[:n_pages], refs[n_pages:2 * n_pages]
    out_ref, h_ref, m_ref, l_ref, acc_ref, xs_ref = refs[2 * n_pages:]
    s_idx = pl.program_id(1)

    last = s_idx == pl.num_programs(1) - 1
    nrow = qc_ref.shape[2]
    cols = k_refs[0].shape[1] // 2
    n_near = bnear_ref.shape[0]
    nt = (((1,), (1,)), ((), ()))
    qcs = (qc_ref[0, 0], qc_ref[0, 1])

    @pl.when(s_idx == 0)
    def _():
        m_ref[...] = jnp.full(m_ref.shape, NEG_INF, F32)
        l_ref[...] = jnp.zeros(l_ref.shape, F32)
        acc_ref[...] = jnp.zeros(acc_ref.shape, F32)

    bfar = bfar_ref[...]
    bias = [bfar] * (n_pages - n_near) + [jnp.where(last, bnear_ref[i], bfar) for i in range(n_near)]
    group = max(n_pages // DEC_GROUPS, 1)
    for lo in range(0, n_pages, group):
        pages = range(lo, min(lo + group, n_pages))
        s_maps = []
        for c in range(2):
            parts = [lax.dot_general(qcs[c], k_refs[i][0, pl.ds(c, cols, stride=2), :].astype(BF16), nt,
                                     preferred_element_type=F32) + bias[i] for i in pages]
            s_maps.append(jnp.concatenate(parts, axis=1))
        s_grp = jnp.concatenate(s_maps, axis=0)

        def pv_grp(p, pages=pages):
            out = None
            for n, i in enumerate(pages):
                term = jnp.dot(p[:, n * cols:(n + 1) * cols], v_refs[i][0].astype(BF16), preferred_element_type=F32)
                out = term if out is None else out + term
            return out

        _softmax_update(s_grp, _block_max(s_grp), m_ref, l_ref, acc_ref, pv_grp)

    blk = pl.program_id(0) * pl.num_programs(1) + s_idx
    hist = jnp.where(blk % conv_blocks_per_seq == 0, 0.0, gprev_ref[0])
    h_ref[0] = _conv_rows(hist, gcur_ref[0], xs_ref, cw_ref, cb_ref, lng_ref, lnb_ref, sza_ref[0].astype(F32))

    @pl.when(last)
    def _():
        s_new = jnp.concatenate([lax.dot_general(qcs[c], knew_ref[0, c], nt, preferred_element_type=F32) + bnew_ref[...]
                                 for c in range(2)], axis=0)
        row = lax.broadcasted_iota(jnp.int32, s_new.shape, 0)
        col = lax.broadcasted_iota(jnp.int32, s_new.shape, 1)
        valid = (col % N_HEADS == (row % nrow) // t_new) & (col // N_HEADS <= row % t_new)
        s_new = jnp.where(valid, s_new, NEG_INF)
        _softmax_update(s_new, _block_max(s_new), m_ref, l_ref, acc_ref,
                        lambda p: jnp.dot(p, vnew_ref[0], preferred_element_type=F32))

        lam = _lambda_value(lq1_ref, lk1_ref, lq2_ref, lk2_ref)
        o = acc_ref[...] / jnp.sum(l_ref[...], axis=-1, keepdims=True)
        out_ref[0] = _subln_gate(o[0:nrow] - lam * o[nrow:2 * nrow], sub_ref[...], szb_ref[0])


def _decode_bias(bdl, past, page, tn):
    nrow = N_HEADS * tn
    same_head = jnp.eye(N_HEADS, dtype=bool)[:, None, None, :]
    n_far = (past - MAX_DISTANCE + 1) // page
    n_near = past // page - n_far
    far = jnp.broadcast_to(jnp.where(same_head, bdl[:, MAX_DISTANCE][:, None, None, None], NEG_INF),
                           (N_HEADS, tn, page, N_HEADS)).reshape(nrow, page * N_HEADS)
    span = n_near * page
    near = jnp.stack([jnp.flip(bdl[:, qi + 1:qi + 1 + span], axis=1) for qi in range(tn)], axis=1)
    near = jnp.where(same_head, near[..., None], NEG_INF).reshape(nrow, n_near, page * N_HEADS)
    new_tok = LANES // N_HEADS
    dist_new = np.maximum(np.arange(tn)[:, None] - np.arange(new_tok)[None, :], 0)
    new = jnp.where(same_head, bdl[:, dist_new][..., None], 0.0).reshape(nrow, LANES)
    return far, jnp.transpose(near, (1, 0, 2)), new


def _attn_decode(page_table, cache_kr, cache_vr, qc, bias_far, bias_near, k_new, v_new, bias_new, szb, sub,
                 lq1, lk1, lq2, lk2, g, sza, conv_w, conv_b, clng, clnb):
    nb, n_pt = page_table.shape
    _, k_rows, _ = cache_kr.shape
    _, v_rows, _ = cache_vr.shape
    nrow = qc.shape[2]
    t_new = nrow // N_HEADS
    npg = DEC_PAGES
    assert n_pt % npg == 0 and k_rows == 2 * v_rows and bias_near.shape[0] <= npg
    new_rows = k_new.shape[2]

    def page_spec(rows, width, i):
        return pl.BlockSpec((1, rows, width), lambda b, s, pt, i=i: (pt[b, s * npg + i], 0, 0))

    ns = n_pt // npg
    bp, t, c = g.shape
    rb = bp * t // (nb * ns)
    assert rb * nb * ns == bp * t and rb % V7X_SUBLANES == 0 and rb >= CONV_WIDTH - 1 and t % rb == 0
    cps = t // rb
    cblk = lambda b, s, pt: ((b * ns + s) // cps, (b * ns + s) % cps, 0)
    cprev = lambda b, s, pt: ((b * ns + s) // cps, jnp.maximum((b * ns + s) % cps - 1, 0), 0)
    cconst = lambda shape: pl.BlockSpec(shape, lambda b, s, pt: (0,) * len(shape))
    vec = pl.BlockSpec((1, HEAD_DIM), lambda b, s, pt: (0, 0))
    in_specs = [
        pl.BlockSpec((1, 2, nrow, HEAD_DIM), lambda b, s, pt: (b, 0, 0, 0)),
        pl.BlockSpec(bias_far.shape, lambda b, s, pt: (0, 0)),
        pl.BlockSpec(bias_near.shape, lambda b, s, pt: (0, 0, 0)),
        pl.BlockSpec((1, 2, new_rows, HEAD_DIM), lambda b, s, pt: (b, 0, 0, 0)),
        pl.BlockSpec((1, new_rows, V_DIM), lambda b, s, pt: (b, 0, 0)),
        pl.BlockSpec((nrow, new_rows), lambda b, s, pt: (0, 0)),
        pl.BlockSpec((1, nrow, V_DIM), lambda b, s, pt: (b, 0, 0)),
        pl.BlockSpec((1, V_DIM), lambda b, s, pt: (0, 0)),
        vec, vec, vec, vec,
        pl.BlockSpec((1, rb, c), cblk), pl.BlockSpec((1, rb, c), cprev), pl.BlockSpec((1, rb, c), cblk),
        cconst((CONV_WIDTH, c)), cconst((1, c)), cconst((1, c)), cconst((1, c)),
    ] + [page_spec(k_rows, HEAD_DIM, i) for i in range(npg)] + [page_spec(v_rows, V_DIM, i) for i in range(npg)]
    page_bytes = k_rows * HEAD_DIM * 4
    vmem = 2 * 2 * npg * page_bytes + 10 * 2 * nrow * npg * v_rows * 4 + 24 * rb * c * 4
    return pl.pallas_call(
        functools.partial(_attn_decode_kernel, n_pages=npg, t_new=t_new, conv_blocks_per_seq=cps),
        grid_spec=pltpu.PrefetchScalarGridSpec(
            num_scalar_prefetch=1,
            grid=(nb, ns),
            in_specs=in_specs,
            out_specs=[pl.BlockSpec((1, nrow, V_DIM), lambda b, s, pt: (b, 0, 0)), pl.BlockSpec((1, rb, c), cblk)],
            scratch_shapes=[pltpu.VMEM((2 * nrow, LANES), F32), pltpu.VMEM((2 * nrow, LANES), F32),
                            pltpu.VMEM((2 * nrow, V_DIM), F32), pltpu.VMEM((V7X_SUBLANES, 2 * rb, c), F32)],
        ),
        out_shape=[jax.ShapeDtypeStruct((nb, nrow, V_DIM), BF16), jax.ShapeDtypeStruct((bp, t, c), BF16)],
        compiler_params=_cparams(("arbitrary", "arbitrary"), vmem),
        name="attn_decode",
    )(page_table, qc, bias_far, bias_near, k_new, v_new, bias_new, szb, sub, lq1, lk1, lq2, lk2,
      g, g, sza, conv_w, conv_b, clng, clnb,
      *([cache_kr] * npg), *([cache_vr] * npg))


def _final_kernel(hb_ref, ma_ref, sgb_ref, x_ref, p_ref, wpb_ref, wout_ref, wgate_ref, wple_ref, lng_ref, lnb_ref, y_ref):
    b_proj = jnp.dot(hb_ref[...], wpb_ref[...], preferred_element_type=F32)
    merged = ma_ref[...] + sgb_ref[...] * b_proj
    z = ALPHA * x_ref[...] + jnp.dot(merged.astype(BF16), wout_ref[...], preferred_element_type=F32)
    mu = jnp.mean(z, axis=-1, keepdims=True)
    zc = z - mu
    var = jnp.mean(zc * zc, axis=-1, keepdims=True)
    h = zc * lax.rsqrt(var + LN_EPS) * lng_ref[...] + lnb_ref[...]
    gate = jax.nn.sigmoid(jnp.dot(h.astype(BF16), wgate_ref[...], preferred_element_type=F32))
    pe = jnp.dot(p_ref[...].astype(BF16), wple_ref[...], preferred_element_type=F32)
    y_ref[...] = h + gate * pe


def _final(hb, ma, sgb, sgb_block, x, p, wpb_bf, wout_bf, wgate_bf, wple_bf, lng, lnb, tm):
    m, d = x.shape
    pd = p.shape[1]
    row = lambda w: pl.BlockSpec((tm, w), lambda i: (i, 0))
    const = lambda r, c: pl.BlockSpec((r, c), lambda i: (0, 0), pipeline_mode=pl.Buffered(1))
    vmem = 2 * tm * (d * 2 + 4 * d * 4 + pd * 4) + (3 * d * d + pd * d) * 2 + 6 * tm * d * 4
    return pl.pallas_call(
        _final_kernel,
        grid=(m // tm,),
        in_specs=[row(d), row(d), pl.BlockSpec((tm, d), lambda i: (i, sgb_block)), row(d), row(pd),
                  const(d, d), const(d, d), const(d, d), const(pd, d), const(1, d), const(1, d)],
        out_specs=row(d),
        out_shape=jax.ShapeDtypeStruct((m, d), F32),
        compiler_params=_cparams(("arbitrary",), vmem),
        name=f"final_m{m}",
    )(hb, ma, sgb, x, p, wpb_bf, wout_bf, wgate_bf, wple_bf, lng, lnb)


def kernel(x_prompt, x_sample, p_prompt, p_sample, cache_k, cache_v, state_conv, page_table, w_in, conv_w, conv_b, conv_ln_g, conv_ln_b, w_proj_a, lambda_q1, lambda_k1, lambda_q2, lambda_k2, subln_w, w_proj_b, w_out, ln_g, ln_b, w_ple_proj, w_ple_gate, rel_bias):
    assert w_in.shape[0] == DEPTH
    b, t, d = x_prompt.shape
    nb, tn, _ = x_sample.shape
    c = conv_w.shape[-1]
    n_maps = 2 * N_HEADS
    qk_w = n_maps * HEAD_DIM
    att_w = N_HEADS * V_DIM
    pd = p_prompt.shape[-1]
    n_phys, page = cache_k.shape[1], cache_k.shape[2]
    past = page_table.shape[1] * page

    cw, cb = conv_w[0], conv_b[0].reshape(1, c)
    clng, clnb = conv_ln_g[0].reshape(1, c), conv_ln_b[0].reshape(1, c)
    lng, lnb = ln_g[0].reshape(1, d), ln_b[0].reshape(1, d)
    sub = subln_w[0].reshape(1, V_DIM)
    lq1, lk1, lq2, lk2 = (v[0].reshape(1, HEAD_DIM) for v in (lambda_q1, lambda_k1, lambda_q2, lambda_k2))
    bdl = _bias_by_distance(rel_bias, MAX_DISTANCE + 2 * LANES + page + tn) * LOG2E

    ms = nb * tn
    xs = jnp.transpose(x_sample, (1, 0, 2)).reshape(ms, d)
    w_bf, (ua_s, ug_s, sza_s, q_s, k_s, k_s_bf, v_s, v_s_bf, szb_s, sga_s, sgb_s) = _in_projection_small(
        xs.astype(BF16), w_in[0], c, qk_w, att_w, d)

    xp = x_prompt.reshape(b * t, d)
    xp_bf, g, sza, q_bf, k_p, k_bf, v_p, v_bf, szb, sgb, ga_col0 = _in_projection(xp, w_bf, PROJ_TM, c, qk_w, att_w, d)
    g3 = g.reshape(b, t, c)

    nrow = N_HEADS * tn
    heads = lambda a, w: a.reshape(tn, nb, N_HEADS, *w)
    qc = jnp.transpose(heads(q_s, (2, HEAD_DIM)), (1, 3, 2, 0, 4)).reshape(nb, 2, nrow, HEAD_DIM)
    k_new = jnp.transpose(heads(k_s_bf, (2, HEAD_DIM)), (1, 3, 0, 2, 4)).reshape(nb, 2, nrow, HEAD_DIM)
    k_new = jnp.pad(k_new, ((0, 0), (0, 0), (0, LANES - nrow), (0, 0)))
    v_new = jnp.transpose(heads(v_s_bf, (V_DIM,)), (1, 0, 2, 3)).reshape(nb, nrow, V_DIM)
    v_new = jnp.pad(v_new, ((0, 0), (0, LANES - nrow), (0, 0)))
    szb_hq = jnp.transpose(heads(szb_s, (V_DIM,)), (1, 2, 0, 3)).reshape(nb, nrow, V_DIM)
    bias_far, bias_near, bias_new = _decode_bias(bdl, past, page, tn)
    hq, h_conv = _attn_decode(page_table, cache_k.reshape(n_phys, page * n_maps, HEAD_DIM),
                              cache_v.reshape(n_phys, page * N_HEADS, V_DIM), qc, bias_far, bias_near, k_new, v_new,
                              bias_new, szb_hq, sub, lq1, lk1, lq2, lk2, g3, sza.reshape(b, t, c), cw, cb, clng, clnb)
    hb_s = jnp.transpose(hq.reshape(nb, N_HEADS, tn, V_DIM), (2, 0, 1, 3)).reshape(ms, att_w)

    hb, wpa_bf, wpb_bf, wout_bf, wgate_bf, wple_bf = _attn_prompt(
        q_bf.reshape(b, t, qk_w), k_bf.reshape(b, t, qk_w), v_bf.reshape(b, t, att_w), szb.reshape(b, t, att_w),
        bdl, sub, lq1, lk1, lq2, lk2, (w_proj_a[0], w_proj_b[0], w_out[0], w_ple_gate[0], w_ple_proj[0]))
    ma_s, g_s_tm = _conv_sample(jnp.transpose(state_conv[0], (1, 0, 2)), ua_s.reshape(tn, nb, c), ug_s.reshape(tn, nb, c),
                                cw, cb, clng, clnb, sza_s, sga_s, wpa_bf)
    y_s = _final(hb_s, ma_s, sgb_s, 0, xs, jnp.transpose(p_sample[0], (1, 0, 2)).reshape(ms, pd),
                 wpb_bf, wout_bf, wgate_bf, wple_bf, lng, lnb, ms)

    ma = _proj_ga(xp_bf, w_bf, ga_col0, h_conv.reshape(b * t, c), wpa_bf, PROJ_TM)
    y_p = _final(hb.reshape(b * t, att_w), ma, sgb, 0, xp, p_prompt[0].reshape(b * t, pd),
                 wpb_bf, wout_bf, wgate_bf, wple_bf, lng, lnb, FINAL_TM)

    batch_major = lambda a: jnp.transpose(a.reshape(tn, nb, *a.shape[1:]), (1, 0) + tuple(range(2, a.ndim + 1)))
    conv_prompt = g3[:, t - (CONV_WIDTH - 1):, :]
    conv_sample = jnp.concatenate([state_conv[0][:, tn:, :], jnp.transpose(g_s_tm, (1, 0, 2))], axis=1)
    return (y_p.reshape(b, t, d), batch_major(y_s),
            k_p.reshape(1, b, t, n_maps, HEAD_DIM), v_p.reshape(1, b, t, N_HEADS, V_DIM), conv_prompt[None],
            batch_major(k_s)[None], batch_major(v_s.reshape(ms, N_HEADS, V_DIM))[None], conv_sample[None])
```

```python
import functools
import math

import numpy as np
import jax
import jax.numpy as jnp
from jax import lax
from jax.experimental import pallas as pl
from jax.experimental.pallas import tpu as pltpu

F32 = jnp.float32
BF16 = jnp.bfloat16

N_HEADS = 8
HEAD_DIM = 128
V_DIM = 2 * HEAD_DIM
CONV_WIDTH = 31
N_BUCKETS = 32
MAX_DISTANCE = 128
LN_EPS = 1e-5
NEG_INF = -1e30
DEPTH = 1
ALPHA = (2 * DEPTH) ** 0.25
LAM_INIT = 0.8 - 0.6 * math.exp(-0.3 * 0)
QK_SCALE = HEAD_DIM ** -0.5
LOG2E = math.log2(math.e)

V7X_VMEM_BYTES = 64 * 1024 * 1024
V7X_SUBLANES = 8
BF16_ROWS = 2 * V7X_SUBLANES
LANES = 128
MIB = 1024 * 1024
VMEM_HEADROOM_BYTES = 6 * MIB

PROJ_TN = 1024
PROJ_TM = 1024
GLU_TM = 512
ATT_TQ = 512
ATT_TK = 512
DEC_PAGES = 8
DEC_GROUPS = 4
FINAL_TM = 512


def _cparams(sem, block_bytes):
    limit = min(block_bytes + VMEM_HEADROOM_BYTES, V7X_VMEM_BYTES - VMEM_HEADROOM_BYTES)
    return pltpu.CompilerParams(dimension_semantics=sem, vmem_limit_bytes=limit)


def _proj_kernel(x_ref, *refs, n_w, epilogue):
    w_refs, out_refs = refs[:n_w], refs[n_w:]
    x = x_ref[...]
    accs = [jnp.dot(x, w[...], preferred_element_type=F32) for w in w_refs]
    for o_ref, o in zip(out_refs, epilogue(*accs)):
        o_ref[...] = o.astype(o_ref.dtype).reshape(o_ref.shape)


def _proj(name, x_bf, w_bf, col_starts, ncols, epilogue, out_dtypes, tm, head_width=None):
    m, k = x_bf.shape
    tn = PROJ_TN
    n_w = len(col_starts)
    in_specs = [pl.BlockSpec((tm, k), lambda j, i: (i, 0))]
    for c0 in col_starts:
        assert c0 % tn == 0
        in_specs.append(pl.BlockSpec((k, tn), lambda j, i, c0=c0: (0, c0 // tn + j)))
    out_specs = [pl.BlockSpec((tm, tn), lambda j, i: (i, j)) for _ in out_dtypes]
    out_shape = [jax.ShapeDtypeStruct((m, ncols), dt) for dt in out_dtypes]
    if head_width is not None:
        assert (tn // head_width) % V7X_SUBLANES == 0
        out_specs[0] = pl.BlockSpec((tm, tn // head_width, head_width), lambda j, i: (i, j, 0))
        out_shape[0] = jax.ShapeDtypeStruct((m, ncols // head_width, head_width), out_dtypes[0])
    vmem = 2 * (tm * k * 2 + n_w * k * tn * 2 + sum(tm * tn * jnp.dtype(d).itemsize for d in out_dtypes))
    vmem += n_w * tm * tn * 4
    return pl.pallas_call(
        functools.partial(_proj_kernel, n_w=n_w, epilogue=epilogue),
        grid=(ncols // tn, m // tm),
        in_specs=in_specs, out_specs=out_specs, out_shape=out_shape,
        compiler_params=_cparams(("arbitrary", "arbitrary"), vmem),
        name=f"proj_{name}_m{m}",
    )(x_bf, *([w_bf] * n_w))


def _sigmoid(z):
    return 0.5 * jnp.tanh(0.5 * z) + 0.5


def _silu(z):
    return z * _sigmoid(z)


def _ep_silu(z):
    return (_silu(z),)


def _ep_sigmoid(z):
    return (_sigmoid(z),)


def _ep_query(z):
    return (z * (QK_SCALE * LOG2E),)


def _ep_copy2(z):
    return (z, z)


def _glu_cast_kernel(x_ref, wa_ref, wg_ref, wz_ref, g_ref, sza_ref, xbf_ref):
    xb = x_ref[...].astype(BF16)
    xbf_ref[...] = xb
    ua = jnp.dot(xb, wa_ref[...], preferred_element_type=F32)
    ug = jnp.dot(xb, wg_ref[...], preferred_element_type=F32)
    g_ref[...] = ua * _sigmoid(ug)
    sza_ref[...] = _silu(jnp.dot(xb, wz_ref[...], preferred_element_type=F32)).astype(sza_ref.dtype)


def _proj_glu_cast(x_f32, w_bf, c_conv):
    m, k = x_f32.shape
    tm, tn = GLU_TM, PROJ_TN
    assert c_conv == tn and m % tm == 0
    vmem = 2 * (tm * k * 4 + 3 * k * tn * 2 + 2 * tm * tn * 4 + tm * k * 2) + 3 * tm * tn * 4 + tm * k * 2
    wspec = lambda n: pl.BlockSpec((k, tn), lambda i: (0, n))
    return pl.pallas_call(
        _glu_cast_kernel,
        grid=(m // tm,),
        in_specs=[pl.BlockSpec((tm, k), lambda i: (i, 0)), wspec(0), wspec(1), wspec(2)],
        out_specs=[pl.BlockSpec((tm, tn), lambda i: (i, 0)), pl.BlockSpec((tm, tn), lambda i: (i, 0)),
                   pl.BlockSpec((tm, k), lambda i: (i, 0))],
        out_shape=[jax.ShapeDtypeStruct((m, c_conv), F32), jax.ShapeDtypeStruct((m, c_conv), BF16),
                   jax.ShapeDtypeStruct((m, k), BF16)],
        compiler_params=_cparams(("arbitrary",), vmem),
        name="proj_glu_cast",
    )(x_f32, w_bf, w_bf, w_bf)


def _proj_value_kernel(x_ref, wa_ref, wb_ref, o_ref, obf_ref):
    x = x_ref[...]
    z = jnp.concatenate([jnp.dot(x, wa_ref[...], preferred_element_type=F32),
                         jnp.dot(x, wb_ref[...], preferred_element_type=F32)], axis=1)
    o_ref[...] = z.reshape(o_ref.shape)
    obf_ref[...] = z.astype(BF16)


def _proj_value(x_bf, w_bf, col0, att_w):
    m, k = x_bf.shape
    tm, tn = GLU_TM, PROJ_TN
    assert att_w == 2 * tn == N_HEADS * V_DIM and col0 % tn == 0 and m % tm == 0
    vmem = 2 * (tm * k * 2 + 2 * k * tn * 2 + tm * att_w * 6) + 2 * tm * att_w * 4
    return pl.pallas_call(
        _proj_value_kernel,
        grid=(m // tm,),
        in_specs=[pl.BlockSpec((tm, k), lambda i: (i, 0)), pl.BlockSpec((k, tn), lambda i: (0, col0 // tn)),
                  pl.BlockSpec((k, tn), lambda i: (0, col0 // tn + 1))],
        out_specs=[pl.BlockSpec((tm, N_HEADS, V_DIM), lambda i: (i, 0, 0)), pl.BlockSpec((tm, att_w), lambda i: (i, 0))],
        out_shape=[jax.ShapeDtypeStruct((m, N_HEADS, V_DIM), F32), jax.ShapeDtypeStruct((m, att_w), BF16)],
        compiler_params=_cparams(("arbitrary",), vmem),
        name="proj_value",
    )(x_bf, w_bf, w_bf)


def _in_projection(x_f32, w_bf, tm, c_conv, qk_w, att_w, d_model):
    g, sza, x_bf = _proj_glu_cast(x_f32, w_bf, c_conv)
    o = 3 * c_conv
    (q_bf,) = _proj("q", x_bf, w_bf, (o,), qk_w, _ep_query, (BF16,), tm)
    o += qk_w
    k, k_bf = _proj("k", x_bf, w_bf, (o,), qk_w, _ep_copy2, (F32, BF16), tm, head_width=HEAD_DIM)
    o += qk_w
    v, v_bf = _proj_value(x_bf, w_bf, o, att_w)
    o += att_w
    (szb,) = _proj("zb", x_bf, w_bf, (o,), att_w, _ep_silu, (F32,), tm)
    o += att_w
    (sgb,) = _proj("gb", x_bf, w_bf, (o + d_model,), d_model, _ep_sigmoid, (F32,), tm)
    return x_bf, g, sza, q_bf, k, k_bf, v, v_bf, szb, sgb, o


def _proj_ga_kernel(x_ref, w_ref, h_ref, wpa_ref, o_ref):
    gate = jax.nn.sigmoid(jnp.dot(x_ref[...], w_ref[...], preferred_element_type=F32))
    o_ref[...] = gate * jnp.dot(h_ref[...], wpa_ref[...], preferred_element_type=F32)


def _proj_ga(x_bf, w_bf, col0, h_bf, wpa_bf, tm):
    m, k = x_bf.shape
    c, d = wpa_bf.shape
    tn = PROJ_TN
    assert col0 % tn == 0 and d % tn == 0
    vmem = 2 * (tm * k * 2 + k * tn * 2 + tm * c * 2 + c * tn * 2 + tm * tn * 4) + 2 * tm * tn * 4
    return pl.pallas_call(
        _proj_ga_kernel,
        grid=(d // tn, m // tm),
        in_specs=[pl.BlockSpec((tm, k), lambda j, i: (i, 0)), pl.BlockSpec((k, tn), lambda j, i: (0, col0 // tn + j)),
                  pl.BlockSpec((tm, c), lambda j, i: (i, 0)), pl.BlockSpec((c, tn), lambda j, i: (0, j))],
        out_specs=pl.BlockSpec((tm, tn), lambda j, i: (i, j)),
        out_shape=jax.ShapeDtypeStruct((m, d), F32),
        compiler_params=_cparams(("arbitrary", "arbitrary"), vmem),
        name="proj_ga",
    )(x_bf, w_bf, h_bf, wpa_bf)


def _proj_small_kernel(x_ref, w_ref, *refs, groups):
    j = pl.program_id(0)
    wbf_ref, out_refs = refs[-1], refs[:-1]
    w = w_ref[...].astype(BF16)
    wbf_ref[...] = w
    z = jnp.dot(x_ref[...], w, preferred_element_type=F32)
    for o_ref, (lo, hi, fn) in zip(out_refs, groups):
        @pl.when((j >= lo) & (j < hi))
        def _(o_ref=o_ref, fn=fn):
            o_ref[...] = fn(z).astype(o_ref.dtype).reshape(o_ref.shape)


def _in_projection_small(x_bf, w_f32, c_conv, qk_w, att_w, d_model):
    m, k = x_bf.shape
    tn = PROJ_TN
    sizes = (c_conv, c_conv, c_conv, qk_w, qk_w, att_w, att_w, d_model, d_model)
    starts = np.concatenate([[0], np.cumsum(sizes)])
    assert all(s % tn == 0 for s in sizes)
    n_in = int(starts[-1])
    ident = lambda z: z
    query = lambda z: z * (QK_SCALE * LOG2E)
    outs = [(0, ident, F32, None), (1, ident, F32, None), (2, _silu, F32, None), (3, query, BF16, None),
            (4, ident, F32, HEAD_DIM), (4, ident, BF16, None), (5, ident, F32, None), (5, ident, BF16, None),
            (6, _silu, F32, None), (7, _sigmoid, F32, None), (8, _sigmoid, F32, None)]
    groups, out_specs, out_shape = [], [], []
    for grp, fn, dt, hw in outs:
        lo, hi = int(starts[grp]) // tn, int(starts[grp + 1]) // tn
        groups.append((lo, hi, fn))
        pin = lambda j, lo=lo, hi=hi: jnp.clip(j - lo, 0, hi - lo - 1)
        if hw is None:
            out_specs.append(pl.BlockSpec((m, tn), lambda j, pin=pin: (0, pin(j))))
            out_shape.append(jax.ShapeDtypeStruct((m, sizes[grp]), dt))
        else:
            assert (tn // hw) % V7X_SUBLANES == 0
            out_specs.append(pl.BlockSpec((m, tn // hw, hw), lambda j, pin=pin: (0, pin(j), 0)))
            out_shape.append(jax.ShapeDtypeStruct((m, sizes[grp] // hw, hw), dt))
    out_specs.append(pl.BlockSpec((k, tn), lambda j: (0, j)))
    out_shape.append(jax.ShapeDtypeStruct((k, n_in), BF16))
    vmem = 2 * (m * k * 2 + k * tn * 6 + len(outs) * m * tn * 4) + k * tn * 2 + 4 * m * tn * 4
    res = pl.pallas_call(
        functools.partial(_proj_small_kernel, groups=tuple(groups)),
        grid=(n_in // tn,),
        in_specs=[pl.BlockSpec((m, k), lambda j: (0, 0)), pl.BlockSpec((k, tn), lambda j: (0, j))],
        out_specs=out_specs, out_shape=out_shape,
        compiler_params=_cparams(("arbitrary",), vmem),
        name="proj_small",
    )(x_bf, w_f32)
    return res[-1], tuple(res[:-1])


def _ln_swish(acc, lng, lnb):
    mu = jnp.mean(acc, axis=-1, keepdims=True)
    xc = acc - mu
    var = jnp.mean(xc * xc, axis=-1, keepdims=True)
    y = xc * lax.rsqrt(var + LN_EPS) * lng + lnb
    return _silu(y)


def _conv_sample_kernel(state_ref, ua_ref, ug_ref, cw_ref, cb_ref, lng_ref, lnb_ref, sza_ref, sga_ref, wpa_ref,
                        out_ref, g_ref, hs_ref):
    n_state, nb, _ = state_ref.shape
    t_new = ua_ref.shape[0]
    g_ref[...] = ua_ref[...] * _sigmoid(ug_ref[...])
    window = lambda r: state_ref[r] if r < n_state else g_ref[r - n_state]
    for t in range(t_new):
        acc = jnp.broadcast_to(cb_ref[...], (nb, cb_ref.shape[-1]))
        for j in range(CONV_WIDTH):
            acc = acc + cw_ref[j:j + 1, :] * window(t + j)
        h = _ln_swish(acc, lng_ref[...], lnb_ref[...]) * sza_ref[t * nb:(t + 1) * nb, :]
        hs_ref[t * nb:(t + 1) * nb, :] = h.astype(BF16)
    out_ref[...] = sga_ref[...] * jnp.dot(hs_ref[...], wpa_ref[...], preferred_element_type=F32)


def _conv_sample(state_tm, ua_tm, ug_tm, conv_w, conv_b, lng, lnb, sza_tm, sga_tm, wpa_bf):
    t_new, nb, c = ua_tm.shape
    d = wpa_bf.shape[1]
    assert state_tm.shape[0] == CONV_WIDTH - 1
    return pl.pallas_call(
        _conv_sample_kernel,
        out_shape=[jax.ShapeDtypeStruct((t_new * nb, d), F32), jax.ShapeDtypeStruct((t_new, nb, c), F32)],
        scratch_shapes=[pltpu.VMEM((t_new * nb, c), BF16)],
        compiler_params=_cparams(None, 4 * (state_tm.size + 3 * ua_tm.size + 3 * sga_tm.size) + 2 * wpa_bf.size),
        name="conv_sample",
    )(state_tm, ua_tm, ug_tm, conv_w, conv_b, lng, lnb, sza_tm, sga_tm, wpa_bf)


def _lambda_value(lq1_ref, lk1_ref, lq2_ref, lk2_ref):
    s1 = jnp.sum(lq1_ref[...] * lk1_ref[...], axis=-1, keepdims=True)
    s2 = jnp.sum(lq2_ref[...] * lk2_ref[...], axis=-1, keepdims=True)
    return jnp.exp(s1) - jnp.exp(s2) + LAM_INIT


def _subln_gate(o, sub, szb):
    r = o * lax.rsqrt(jnp.mean(o * o, axis=-1, keepdims=True) + LN_EPS) * sub * (1.0 - LAM_INIT)
    return (r * szb).astype(BF16)


def _lane_blocks(x):
    return [x[:, i * LANES:(i + 1) * LANES] for i in range(x.shape[1] // LANES)]


def _block_max(s):
    return functools.reduce(jnp.maximum, _lane_blocks(s))


def _softmax_update(s, bmax, m_ref, l_ref, acc_ref, pv_fn):
    m_old = m_ref[...]
    m_new = jnp.maximum(m_old, jnp.max(bmax, axis=-1, keepdims=True))
    alpha = jnp.exp2(m_old - m_new)
    p_blocks = [jnp.exp2(sb - m_new) for sb in _lane_blocks(s)]
    l_ref[...] = alpha * l_ref[...] + functools.reduce(jnp.add, p_blocks)
    pv = pv_fn(jnp.concatenate(p_blocks, axis=1).astype(BF16))
    acc_ref[...] = jnp.concatenate([alpha] * (pv.shape[1] // LANES), axis=1) * acc_ref[...] + pv
    m_ref[...] = m_new


def _bias_by_distance(rel_bias, n_max):
    n = jnp.arange(n_max)
    max_exact = N_BUCKETS // 2
    large = max_exact + (jnp.log(jnp.maximum(n, 1).astype(F32) / max_exact)
                         / math.log(MAX_DISTANCE / max_exact) * (N_BUCKETS - max_exact)).astype(jnp.int32)
    large = jnp.minimum(large, N_BUCKETS - 1)
    bucket = jnp.where(n < max_exact, n, large)
    return jnp.transpose(rel_bias[bucket].astype(F32), (1, 0))


def _toeplitz_blocks(bdl):
    assert MAX_DISTANCE <= LANES
    per = 2 * LANES
    n = np.arange(per)
    blocks = []
    for e in range(2):
        idx = np.where(n < LANES, np.maximum(LANES * e - n, 0), LANES * e + per - n)
        w = bdl[:, idx]
        x = jnp.tile(w, (1, LANES))[:, :LANES * (per - 1)].reshape(-1, LANES, per - 1)
        blocks.append(x[:, :, :LANES])
    return jnp.stack(blocks, axis=1)


def _attn_prompt_kernel(cfar_ref, q_ref, k_ref, v_ref, u_ref, szb_ref, sub_ref, lq1_ref, lk1_ref, lq2_ref, lk2_ref,
                        *refs, tq, tk, n_cast):
    cast_in, (out_ref, *cast_out) = refs[:n_cast], refs[n_cast:2 * n_cast + 1]
    strip_ref, s_ref, bm_ref, m_ref, l_ref, acc_ref = refs[2 * n_cast + 1:]
    h = pl.program_id(0)
    qi = pl.program_id(1)
    nbt = q_ref.shape[0]
    far = tq + 2 * LANES

    @pl.when(qi == 0)
    def _():
        strip_ref[...] = jnp.full(strip_ref.shape, cfar_ref[h], F32)
        strip_ref[tq:tq + LANES, :] = u_ref[0, 0]
        strip_ref[tq + LANES:far, :] = u_ref[0, 1]

    m_ref[...] = jnp.full(m_ref.shape, NEG_INF, F32)
    l_ref[...] = jnp.zeros(l_ref.shape, F32)
    acc_ref[...] = jnp.zeros(acc_ref.shape, F32)

    def scores(j, slot):
        off = pl.multiple_of(j * tk, tk)
        r0 = tq + (qi - j) * tk
        bias = [strip_ref[pl.ds(pl.multiple_of(jnp.minimum(r0 - cb * LANES, far), LANES), tq), :]
                for cb in range(tk // LANES)]
        for bi in range(nbt):
            kj = k_ref[bi, pl.ds(off, tk), :]
            q = q_ref[bi]
            for c in range(2):
                qk = lax.dot_general(q[:, c * HEAD_DIM:(c + 1) * HEAD_DIM], kj[:, c * HEAD_DIM:(c + 1) * HEAD_DIM],
                                     (((1,), (1,)), ((), ())), preferred_element_type=F32)
                s_blocks = [sb + bb for sb, bb in zip(_lane_blocks(qk), bias)]
                s_ref[slot, bi, c] = jnp.concatenate(s_blocks, axis=1)
                bm_ref[slot, bi, c] = functools.reduce(jnp.maximum, s_blocks)

    def consume(j, slot, mask):
        off = pl.multiple_of(j * tk, tk)
        for bi in range(nbt):
            vj = v_ref[bi, pl.ds(off, tk), :]
            for c in range(2):
                s = s_ref[slot, bi, c]
                if mask is None:
                    bmax = bm_ref[slot, bi, c]
                else:
                    s = jnp.where(mask, s, NEG_INF)
                    bmax = _block_max(s)
                _softmax_update(s, bmax, m_ref.at[bi, c], l_ref.at[bi, c], acc_ref.at[bi, c],
                                lambda p, vj=vj: jnp.dot(p, vj, preferred_element_type=F32))

    scores(0, 0)

    def body(j, carry):
        consume(j, j % 2, None)
        scores(j + 1, (j + 1) % 2)
        return carry

    lax.fori_loop(0, qi, body, 0)

    row = lax.broadcasted_iota(jnp.int32, (tq, tk), 0)
    col = lax.broadcasted_iota(jnp.int32, (tq, tk), 1)
    consume(qi, qi % 2, col <= row)

    lam = _lambda_value(lq1_ref, lk1_ref, lq2_ref, lk2_ref)
    for bi in range(nbt):
        o1 = acc_ref[bi, 0] / jnp.sum(l_ref[bi, 0], axis=-1, keepdims=True)
        o2 = acc_ref[bi, 1] / jnp.sum(l_ref[bi, 1], axis=-1, keepdims=True)
        out_ref[bi] = _subln_gate(o1 - lam * o2, sub_ref[...], szb_ref[bi])

    for w_ref, o_ref in zip(cast_in, cast_out):
        o_ref[...] = w_ref[...].astype(BF16)


def _attn_prompt(q_bf, k_bf, v_bf, szb, bdl, sub, lq1, lk1, lq2, lk2, weights):
    b, t, _ = q_bf.shape
    tq, tk = ATT_TQ, ATT_TK
    assert tq == tk and tq % LANES == 0 and t % tq == 0
    nq = t // tq
    n_steps = N_HEADS * nq
    cast_specs = []
    for w in weights:
        rows = max(BF16_ROWS, -(-w.shape[0] // n_steps // BF16_ROWS) * BF16_ROWS)
        assert w.shape[0] % rows == 0 and w.shape[0] // rows <= n_steps
        last_blk = w.shape[0] // rows - 1
        cast_specs.append(pl.BlockSpec(
            (rows, w.shape[1]), lambda h, i, last_blk=last_blk: (jnp.minimum(h * nq + i, last_blk), 0)))
    u = _toeplitz_blocks(bdl)
    cfar = bdl[:, MAX_DISTANCE]
    smem = pl.BlockSpec(memory_space=pltpu.SMEM)
    vec = pl.BlockSpec((1, HEAD_DIM), lambda h, i: (0, 0))
    strip_rows = 2 * tq + 2 * LANES
    vmem = (2 * b * (2 * t * V_DIM * 2 + tq * V_DIM * (2 + 4 + 2)) + 4 * LANES * LANES * 4
            + b * 2 * 2 * tq * tk * 4 + (strip_rows + b * (2 * 2 + 2 * 2) * tq) * LANES * 4 + b * 2 * tq * V_DIM * 4
            + 4 * tq * tk * 4)
    return pl.pallas_call(
        functools.partial(_attn_prompt_kernel, tq=tq, tk=tk, n_cast=len(weights)),
        grid=(N_HEADS, nq),
        in_specs=[
            smem,
            pl.BlockSpec((b, tq, V_DIM), lambda h, i: (0, i, h)),
            pl.BlockSpec((b, t, V_DIM), lambda h, i: (0, 0, h)),
            pl.BlockSpec((b, t, V_DIM), lambda h, i: (0, 0, h)),
            pl.BlockSpec((1, 2, LANES, LANES), lambda h, i: (h, 0, 0, 0)),
            pl.BlockSpec((b, tq, V_DIM), lambda h, i: (0, i, h)),
            pl.BlockSpec((1, V_DIM), lambda h, i: (0, 0)),
            vec, vec, vec, vec,
        ] + cast_specs,
        out_specs=[pl.BlockSpec((b, tq, V_DIM), lambda h, i: (0, i, h))] + cast_specs,
        out_shape=[jax.ShapeDtypeStruct((b, t, N_HEADS * V_DIM), BF16)]
        + [jax.ShapeDtypeStruct(w.shape, BF16) for w in weights],
        scratch_shapes=[pltpu.VMEM((strip_rows, LANES), F32), pltpu.VMEM((2, b, 2, tq, tk), F32),
                        pltpu.VMEM((2, b, 2, tq, LANES), F32), pltpu.VMEM((b, 2, tq, LANES), F32),
                        pltpu.VMEM((b, 2, tq, LANES), F32), pltpu.VMEM((b, 2, tq, V_DIM), F32)],
        compiler_params=_cparams(("arbitrary", "arbitrary"), vmem),
        name="attn_prompt",
    )(cfar, q_bf, k_bf, v_bf, u, szb, sub, lq1, lk1, lq2, lk2, *weights)


def _conv_rows(hist, cur, xs_ref, cw_ref, cb_ref, lng_ref, lnb_ref, sza):
    rb = cur.shape[0]
    n = 2 * rb
    xs_ref[0, 0:rb, :] = hist
    xs_ref[0, rb:n, :] = cur
    for r in range(1, V7X_SUBLANES):
        xs_ref[r, 0:n - V7X_SUBLANES, :] = xs_ref[0, r:r + n - V7X_SUBLANES, :]
    first = rb - (CONV_WIDTH - 1)
    acc = jnp.broadcast_to(cb_ref[...], cur.shape)
    for j in range(CONV_WIDTH):
        r, a = (first + j) % V7X_SUBLANES, (first + j) // V7X_SUBLANES
        acc = acc + cw_ref[j:j + 1, :] * xs_ref[r, V7X_SUBLANES * a:V7X_SUBLANES * a + rb, :]
    return (_ln_swish(acc, lng_ref[...], lnb_ref[...]) * sza).astype(BF16)


def _attn_decode_kernel(pt_ref, qc_ref, bfar_ref, bnear_ref, knew_ref, vnew_ref, bnew_ref, szb_ref, sub_ref,
                        lq1_ref, lk1_ref, lq2_ref, lk2_ref, gcur_ref, gprev_ref, sza_ref, cw_ref, cb_ref, lng_ref, lnb_ref,
                        *refs, n_pages, t_new, conv_blocks_per_seq):
    del pt_ref
    k_refs, v_refs = refs[:n_pages], refs[n_pages:2 * n_pages]
    out_ref, h_ref, m_ref, l_ref, acc_ref, xs_ref = refs[2 * n_pages:]
    s_idx = pl.program_id(1)

    last = s_idx == pl.num_programs(1) - 1
    nrow = qc_ref.shape[2]
    cols = k_refs[0].shape[1] // 2
    n_near = bnear_ref.shape[0]
    nt = (((1,), (1,)), ((), ()))
    qcs = (qc_ref[0, 0], qc_ref[0, 1])

    @pl.when(s_idx == 0)
    def _():
        m_ref[...] = jnp.full(m_ref.shape, NEG_INF, F32)
        l_ref[...] = jnp.zeros(l_ref.shape, F32)
        acc_ref[...] = jnp.zeros(acc_ref.shape, F32)

    bfar = bfar_ref[...]
    bias = [bfar] * (n_pages - n_near) + [jnp.where(last, bnear_ref[i], bfar) for i in range(n_near)]
    group = max(n_pages // DEC_GROUPS, 1)
    for lo in range(0, n_pages, group):
        pages = range(lo, min(lo + group, n_pages))
        s_maps = []
        for c in range(2):
            parts = [lax.dot_general(qcs[c], k_refs[i][0, pl.ds(c, cols, stride=2), :].astype(BF16), nt,
                                     preferred_element_type=F32) + bias[i] for i in pages]
            s_maps.append(jnp.concatenate(parts, axis=1))
        s_grp = jnp.concatenate(s_maps, axis=0)

        def pv_grp(p, pages=pages):
            out = None
            for n, i in enumerate(pages):
                term = jnp.dot(p[:, n * cols:(n + 1) * cols], v_refs[i][0].astype(BF16), preferred_element_type=F32)
                out = term if out is None else out + term
            return out

        _softmax_update(s_grp, _block_max(s_grp), m_ref, l_ref, acc_ref, pv_grp)

    blk = pl.program_id(0) * pl.num_programs(1) + s_idx
    hist = jnp.where(blk % conv_blocks_per_seq == 0, 0.0, gprev_ref[0])
    h_ref[0] = _conv_rows(hist, gcur_ref[0], xs_ref, cw_ref, cb_ref, lng_ref, lnb_ref, sza_ref[0].astype(F32))

    @pl.when(last)
    def _():
        s_new = jnp.concatenate([lax.dot_general(qcs[c], knew_ref[0, c], nt, preferred_element_type=F32) + bnew_ref[...]
                                 for c in range(2)], axis=0)
        row = lax.broadcasted_iota(jnp.int32, s_new.shape, 0)
        col = lax.broadcasted_iota(jnp.int32, s_new.shape, 1)
        valid = (col % N_HEADS == (row % nrow) // t_new) & (col // N_HEADS <= row % t_new)
        s_new = jnp.where(valid, s_new, NEG_INF)
        _softmax_update(s_new, _block_max(s_new), m_ref, l_ref, acc_ref,
                        lambda p: jnp.dot(p, vnew_ref[0], preferred_element_type=F32))

        lam = _lambda_value(lq1_ref, lk1_ref, lq2_ref, lk2_ref)
        o = acc_ref[...] / jnp.sum(l_ref[...], axis=-1, keepdims=True)
        out_ref[0] = _subln_gate(o[0:nrow] - lam * o[nrow:2 * nrow], sub_ref[...], szb_ref[0])


def _decode_bias(bdl, past, page, tn):
    nrow = N_HEADS * tn
    same_head = jnp.eye(N_HEADS, dtype=bool)[:, None, None, :]
    n_far = (past - MAX_DISTANCE + 1) // page
    n_near = past // page - n_far
    far = jnp.broadcast_to(jnp.where(same_head, bdl[:, MAX_DISTANCE][:, None, None, None], NEG_INF),
                           (N_HEADS, tn, page, N_HEADS)).reshape(nrow, page * N_HEADS)
    span = n_near * page
    near = jnp.stack([jnp.flip(bdl[:, qi + 1:qi + 1 + span], axis=1) for qi in range(tn)], axis=1)
    near = jnp.where(same_head, near[..., None], NEG_INF).reshape(nrow, n_near, page * N_HEADS)
    new_tok = LANES // N_HEADS
    dist_new = np.maximum(np.arange(tn)[:, None] - np.arange(new_tok)[None, :], 0)
    new = jnp.where(same_head, bdl[:, dist_new][..., None], 0.0).reshape(nrow, LANES)
    return far, jnp.transpose(near, (1, 0, 2)), new


def _attn_decode(page_table, cache_kr, cache_vr, qc, bias_far, bias_near, k_new, v_new, bias_new, szb, sub,
                 lq1, lk1, lq2, lk2, g, sza, conv_w, conv_b, clng, clnb):
    nb, n_pt = page_table.shape
    _, k_rows, _ = cache_kr.shape
    _, v_rows, _ = cache_vr.shape
    nrow = qc.shape[2]
    t_new = nrow // N_HEADS
    npg = DEC_PAGES
    assert n_pt % npg == 0 and k_rows == 2 * v_rows and bias_near.shape[0] <= npg
    new_rows = k_new.shape[2]

    def page_spec(rows, width, i):
        return pl.BlockSpec((1, rows, width), lambda b, s, pt, i=i: (pt[b, s * npg + i], 0, 0))

    ns = n_pt // npg
    bp, t, c = g.shape
    rb = bp * t // (nb * ns)
    assert rb * nb * ns == bp * t and rb % V7X_SUBLANES == 0 and rb >= CONV_WIDTH - 1 and t % rb == 0
    cps = t // rb
    cblk = lambda b, s, pt: ((b * ns + s) // cps, (b * ns + s) % cps, 0)
    cprev = lambda b, s, pt: ((b * ns + s) // cps, jnp.maximum((b * ns + s) % cps - 1, 0), 0)
    cconst = lambda shape: pl.BlockSpec(shape, lambda b, s, pt: (0,) * len(shape))
    vec = pl.BlockSpec((1, HEAD_DIM), lambda b, s, pt: (0, 0))
    in_specs = [
        pl.BlockSpec((1, 2, nrow, HEAD_DIM), lambda b, s, pt: (b, 0, 0, 0)),
        pl.BlockSpec(bias_far.shape, lambda b, s, pt: (0, 0)),
        pl.BlockSpec(bias_near.shape, lambda b, s, pt: (0, 0, 0)),
        pl.BlockSpec((1, 2, new_rows, HEAD_DIM), lambda b, s, pt: (b, 0, 0, 0)),
        pl.BlockSpec((1, new_rows, V_DIM), lambda b, s, pt: (b, 0, 0)),
        pl.BlockSpec((nrow, new_rows), lambda b, s, pt: (0, 0)),
        pl.BlockSpec((1, nrow, V_DIM), lambda b, s, pt: (b, 0, 0)),
        pl.BlockSpec((1, V_DIM), lambda b, s, pt: (0, 0)),
        vec, vec, vec, vec,
        pl.BlockSpec((1, rb, c), cblk), pl.BlockSpec((1, rb, c), cprev), pl.BlockSpec((1, rb, c), cblk),
        cconst((CONV_WIDTH, c)), cconst((1, c)), cconst((1, c)), cconst((1, c)),
    ] + [page_spec(k_rows, HEAD_DIM, i) for i in range(npg)] + [page_spec(v_rows, V_DIM, i) for i in range(npg)]
    page_bytes = k_rows * HEAD_DIM * 4
    vmem = 2 * 2 * npg * page_bytes + 10 * 2 * nrow * npg * v_rows * 4 + 24 * rb * c * 4
    return pl.pallas_call(
        functools.partial(_attn_decode_kernel, n_pages=npg, t_new=t_new, conv_blocks_per_seq=cps),
        grid_spec=pltpu.PrefetchScalarGridSpec(
            num_scalar_prefetch=1,
            grid=(nb, ns),
            in_specs=in_specs,
            out_specs=[pl.BlockSpec((1, nrow, V_DIM), lambda b, s, pt: (b, 0, 0)), pl.BlockSpec((1, rb, c), cblk)],
            scratch_shapes=[pltpu.VMEM((2 * nrow, LANES), F32), pltpu.VMEM((2 * nrow, LANES), F32),
                            pltpu.VMEM((2 * nrow, V_DIM), F32), pltpu.VMEM((V7X_SUBLANES, 2 * rb, c), F32)],
        ),
        out_shape=[jax.ShapeDtypeStruct((nb, nrow, V_DIM), BF16), jax.ShapeDtypeStruct((bp, t, c), BF16)],
        compiler_params=_cparams(("arbitrary", "arbitrary"), vmem),
        name="attn_decode",
    )(page_table, qc, bias_far, bias_near, k_new, v_new, bias_new, szb, sub, lq1, lk1, lq2, lk2,
      g, g, sza, conv_w, conv_b, clng, clnb,
      *([cache_kr] * npg), *([cache_vr] * npg))


def _merge_kernel(hb_ref, ma_ref, sgb_ref, wpb_ref, merged_ref):
    b_proj = jnp.dot(hb_ref[...], wpb_ref[...], preferred_element_type=F32)
    merged_ref[...] = (ma_ref[...] + sgb_ref[...] * b_proj).astype(BF16)


def _final_kernel(merged_ref, x_ref, p_ref, wout_ref, wgate_ref, wple_ref, lng_ref, lnb_ref, y_ref):
    z = ALPHA * x_ref[...] + jnp.dot(merged_ref[...], wout_ref[...], preferred_element_type=F32)
    mu = jnp.mean(z, axis=-1, keepdims=True)
    zc = z - mu
    var = jnp.mean(zc * zc, axis=-1, keepdims=True)
    h = zc * lax.rsqrt(var + LN_EPS) * lng_ref[...] + lnb_ref[...]
    gate = jax.nn.sigmoid(jnp.dot(h.astype(BF16), wgate_ref[...], preferred_element_type=F32))
    pe = jnp.dot(p_ref[...].astype(BF16), wple_ref[...], preferred_element_type=F32)
    y_ref[...] = h + gate * pe


def _final(hb, ma, sgb, sgb_block, x, p, wpb_bf, wout_bf, wgate_bf, wple_bf, lng, lnb, tm):
    m, d = x.shape
    pd = p.shape[1]
    row = lambda w: pl.BlockSpec((tm, w), lambda i: (i, 0))
    const = lambda r, c: pl.BlockSpec((r, c), lambda i: (0, 0), pipeline_mode=pl.Buffered(1))
    merged = pl.pallas_call(
        _merge_kernel,
        grid=(m // tm,),
        in_specs=[row(d), row(d), pl.BlockSpec((tm, d), lambda i: (i, sgb_block)), const(d, d)],
        out_specs=row(d),
        out_shape=jax.ShapeDtypeStruct((m, d), BF16),
        compiler_params=_cparams(("arbitrary",), 2 * tm * (2 * d * 2 + 2 * d * 4) + d * d * 2 + 2 * tm * d * 4),
        name=f"merge_m{m}",
    )(hb, ma, sgb, wpb_bf)
    vmem = 2 * tm * (d * 2 + 2 * d * 4 + pd * 4) + (2 * d * d + pd * d) * 2 + 4 * tm * d * 4
    return pl.pallas_call(
        _final_kernel,
        grid=(m // tm,),
        in_specs=[row(d), row(d), row(pd), const(d, d), const(d, d), const(pd, d), const(1, d), const(1, d)],
        out_specs=row(d),
        out_shape=jax.ShapeDtypeStruct((m, d), F32),
        compiler_params=_cparams(("arbitrary",), vmem),
        name=f"final_m{m}",
    )(merged, x, p, wout_bf, wgate_bf, wple_bf, lng, lnb)


def kernel(x_prompt, x_sample, p_prompt, p_sample, cache_k, cache_v, state_conv, page_table, w_in, conv_w, conv_b, conv_ln_g, conv_ln_b, w_proj_a, lambda_q1, lambda_k1, lambda_q2, lambda_k2, subln_w, w_proj_b, w_out, ln_g, ln_b, w_ple_proj, w_ple_gate, rel_bias):
    assert w_in.shape[0] == DEPTH
    b, t, d = x_prompt.shape
    nb, tn, _ = x_sample.shape
    c = conv_w.shape[-1]
    n_maps = 2 * N_HEADS
    qk_w = n_maps * HEAD_DIM
    att_w = N_HEADS * V_DIM
    pd = p_prompt.shape[-1]
    n_phys, page = cache_k.shape[1], cache_k.shape[2]
    past = page_table.shape[1] * page

    cw, cb = conv_w[0], conv_b[0].reshape(1, c)
    clng, clnb = conv_ln_g[0].reshape(1, c), conv_ln_b[0].reshape(1, c)
    lng, lnb = ln_g[0].reshape(1, d), ln_b[0].reshape(1, d)
    sub = subln_w[0].reshape(1, V_DIM)
    lq1, lk1, lq2, lk2 = (v[0].reshape(1, HEAD_DIM) for v in (lambda_q1, lambda_k1, lambda_q2, lambda_k2))
    bdl = _bias_by_distance(rel_bias, MAX_DISTANCE + 2 * LANES + page + tn) * LOG2E

    ms = nb * tn
    xs = jnp.transpose(x_sample, (1, 0, 2)).reshape(ms, d)
    w_bf, (ua_s, ug_s, sza_s, q_s, k_s, k_s_bf, v_s, v_s_bf, szb_s, sga_s, sgb_s) = _in_projection_small(
        xs.astype(BF16), w_in[0], c, qk_w, att_w, d)

    xp = x_prompt.reshape(b * t, d)
    xp_bf, g, sza, q_bf, k_p, k_bf, v_p, v_bf, szb, sgb, ga_col0 = _in_projection(xp, w_bf, PROJ_TM, c, qk_w, att_w, d)
    g3 = g.reshape(b, t, c)

    nrow = N_HEADS * tn
    heads = lambda a, w: a.reshape(tn, nb, N_HEADS, *w)
    qc = jnp.transpose(heads(q_s, (2, HEAD_DIM)), (1, 3, 2, 0, 4)).reshape(nb, 2, nrow, HEAD_DIM)
    k_new = jnp.transpose(heads(k_s_bf, (2, HEAD_DIM)), (1, 3, 0, 2, 4)).reshape(nb, 2, nrow, HEAD_DIM)
    k_new = jnp.pad(k_new, ((0, 0), (0, 0), (0, LANES - nrow), (0, 0)))
    v_new = jnp.transpose(heads(v_s_bf, (V_DIM,)), (1, 0, 2, 3)).reshape(nb, nrow, V_DIM)
    v_new = jnp.pad(v_new, ((0, 0), (0, LANES - nrow), (0, 0)))
    szb_hq = jnp.transpose(heads(szb_s, (V_DIM,)), (1, 2, 0, 3)).reshape(nb, nrow, V_DIM)
    bias_far, bias_near, bias_new = _decode_bias(bdl, past, page, tn)
    hq, h_conv = _attn_decode(page_table, cache_k.reshape(n_phys, page * n_maps, HEAD_DIM),
                              cache_v.reshape(n_phys, page * N_HEADS, V_DIM), qc, bias_far, bias_near, k_new, v_new,
                              bias_new, szb_hq, sub, lq1, lk1, lq2, lk2, g3, sza.reshape(b, t, c), cw, cb, clng, clnb)
    hb_s = jnp.transpose(hq.reshape(nb, N_HEADS, tn, V_DIM), (2, 0, 1, 3)).reshape(ms, att_w)

    hb, wpa_bf, wpb_bf, wout_bf, wgate_bf, wple_bf = _attn_prompt(
        q_bf.reshape(b, t, qk_w), k_bf.reshape(b, t, qk_w), v_bf.reshape(b, t, att_w), szb.reshape(b, t, att_w),
        bdl, sub, lq1, lk1, lq2, lk2, (w_proj_a[0], w_proj_b[0], w_out[0], w_ple_gate[0], w_ple_proj[0]))
    ma_s, g_s_tm = _conv_sample(jnp.transpose(state_conv[0], (1, 0, 2)), ua_s.reshape(tn, nb, c), ug_s.reshape(tn, nb, c),
                                cw, cb, clng, clnb, sza_s, sga_s, wpa_bf)
    y_s = _final(hb_s, ma_s, sgb_s, 0, xs, jnp.transpose(p_sample[0], (1, 0, 2)).reshape(ms, pd),
                 wpb_bf, wout_bf, wgate_bf, wple_bf, lng, lnb, ms)

    ma = _proj_ga(xp_bf, w_bf, ga_col0, h_conv.reshape(b * t, c), wpa_bf, PROJ_TM)
    y_p = _final(hb.reshape(b * t, att_w), ma, sgb, 0, xp, p_prompt[0].reshape(b * t, pd),
                 wpb_bf, wout_bf, wgate_bf, wple_bf, lng, lnb, FINAL_TM)

    batch_major = lambda a: jnp.transpose(a.reshape(tn, nb, *a.shape[1:]), (1, 0) + tuple(range(2, a.ndim + 1)))
    conv_prompt = g3[:, t - (CONV_WIDTH - 1):, :]
    conv_sample = jnp.concatenate([state_conv[0][:, tn:, :], jnp.transpose(g_s_tm, (1, 0, 2))], axis=1)
    return (y_p.reshape(b, t, d), batch_major(y_s),
            k_p.reshape(1, b, t, n_maps, HEAD_DIM), v_p.reshape(1, b, t, N_HEADS, V_DIM), conv_prompt[None],
            batch_major(k_s)[None], batch_major(v_s.reshape(ms, N_HEADS, V_DIM))[None], conv_sample[None])
```
